```python
import jax, jax.numpy as jnp
from jax import lax
import numpy as np

D_MODEL = 1024
BATCH = 32
SEQ = 2048
DEPTH = 1

N_MEM = 256
DIL_PATTERNS = ((128, 1), (512, 4), (2048, 16))
N_DIL_GROUPS = len(DIL_PATTERNS)
DIL_HEADS = 8
DIL_HEAD_DIM = D_MODEL // 16
DIL_WIDTH = DIL_HEADS * DIL_HEAD_DIM
RET_HEADS = 4
RET_QK_DIM = D_MODEL // 16
RET_V_DIM = 2 * RET_QK_DIM
RET_QK_WIDTH = RET_HEADS * RET_QK_DIM
RET_V_WIDTH = RET_HEADS * RET_V_DIM
RET_CHUNK = 128
MEM_HEADS = 4
MEM_HEAD_DIM = D_MODEL // 8
MEM_WIDTH = MEM_HEADS * MEM_HEAD_DIM
N_BRANCHES = 3
D_FF = 2816
CONV_WIDTH = 3
NORM_EPS = 1e-6
MASK_VALUE = -1e30
DIL_COLS = N_DIL_GROUPS * 3 * DIL_WIDTH
RET_COLS = 2 * RET_QK_WIDTH + 2 * RET_V_WIDTH
IN_COLS = DIL_COLS + RET_COLS + MEM_WIDTH + N_BRANCHES * D_MODEL

kernel_name = "hybrid_dilated_retention_memory_block"


def rms_norm(x, g):
    xf = x.astype(jnp.float32)
    y = xf * lax.rsqrt(jnp.mean(xf * xf, axis=-1, keepdims=True) + NORM_EPS)
    return (y * g.astype(jnp.float32)).astype(x.dtype)


def alibi_slopes(n_heads):
    exps = jnp.arange(1, n_heads + 1, dtype=jnp.float32) * (8.0 / n_heads)
    return jnp.exp2(-exps)


def dilated_window_attention(q, k, v, dilation, half, slopes):
    b, s, h, dh = q.shape
    r = dilation
    n_sub = s // r
    nb = -(-n_sub // half)
    lp = nb * half

    def residue_split(t):
        t = t.reshape(b, n_sub, r, h, dh).transpose(0, 2, 3, 1, 4)
        return jnp.pad(t, ((0, 0), (0, 0), (0, 0), (0, lp - n_sub), (0, 0)))

    def band(t):
        tp = jnp.pad(t, ((0, 0), (0, 0), (0, 0), (half, half), (0, 0))).reshape(b, r, h, nb + 2, half, dh)
        return jnp.concatenate([tp[:, :, :, :-2], tp[:, :, :, 1:-1], tp[:, :, :, 2:]], axis=4)

    qb = residue_split(q).reshape(b, r, h, nb, half, dh)
    kb = band(residue_split(k))
    vb = band(residue_split(v))
    scores = jnp.einsum('brhnqd,brhnkd->brhnqk', qb, kb) * (dh ** -0.5)
    q_idx = jnp.arange(nb)[:, None] * half + jnp.arange(half)[None, :]
    k_idx = jnp.arange(nb)[:, None] * half - half + jnp.arange(3 * half)[None, :]
    rel = k_idx[:, None, :] - q_idx[:, :, None]
    valid = (jnp.abs(rel) <= half) & (k_idx[:, None, :] >= 0) & (k_idx[:, None, :] < n_sub)
    dist = (jnp.abs(rel) * r).astype(jnp.float32)
    scores = scores - slopes[:, None, None, None] * dist[None]
    scores = jnp.where(valid, scores, MASK_VALUE)
    m = jnp.max(scores, axis=-1, keepdims=True)
    p = jnp.exp(scores - m)
    denom = jnp.sum(p, axis=-1)
    o = jnp.einsum('brhnqk,brhnkd->brhnqd', p, vb) / denom[..., None]
    lse = m[..., 0] + jnp.log(denom)
    o = o.reshape(b, r, h, lp, dh)[:, :, :, :n_sub].transpose(0, 3, 1, 2, 4).reshape(b, s, h, dh)
    lse = lse.reshape(b, r, h, lp)[:, :, :, :n_sub].transpose(0, 3, 1, 2).reshape(b, s, h)
    return o, lse


def dilated_mixture_attention(p, q_norm_g, k_norm_g, slopes):
    b, s, _ = p.shape
    p = p.reshape(b, s, N_DIL_GROUPS, 3, DIL_HEADS, DIL_HEAD_DIM)
    outs, lses = [], []
    for g, (window, dilation) in enumerate(DIL_PATTERNS):
        q = rms_norm(p[:, :, g, 0], q_norm_g[g]).astype(jnp.float32)
        k = rms_norm(p[:, :, g, 1], k_norm_g[g]).astype(jnp.float32)
        v = p[:, :, g, 2].astype(jnp.float32)
        o, lse = dilated_window_attention(q, k, v, dilation, window // (2 * dilation), slopes)
        outs.append(o)
        lses.append(lse)
    weights = jax.nn.softmax(jnp.stack(lses), axis=0)
    o = jnp.einsum('gbsh,gbshd->bshd', weights, jnp.stack(outs))
    return o.reshape(b, s, DIL_WIDTH).astype(p.dtype)


def chunkwise_retention(q, k, v, log_gamma, strict):
    b, h, s, dk = q.shape
    dv = v.shape[-1]
    c = RET_CHUNK
    n = s // c
    qc = q.reshape(b, h, n, c, dk)
    kc = k.reshape(b, h, n, c, dk)
    vc = v.reshape(b, h, n, c, dv)
    idx = jnp.arange(c)
    diff = idx[:, None] - idx[None, :]
    mask = (diff > 0) if strict else (diff >= 0)
    decay = jnp.where(mask[None], jnp.exp(log_gamma[:, None, None] * jnp.maximum(diff, 0).astype(jnp.float32)[None]), 0.0)
    inner = jnp.einsum('bhnid,bhnjd->bhnij', qc, kc) * decay[:, None]
    y_inner = jnp.einsum('bhnij,bhnjv->bhniv', inner, vc)
    zeta = jnp.exp(log_gamma[:, None] * (c - 1 - idx).astype(jnp.float32))
    xi = jnp.exp(log_gamma[:, None] * (idx + 1).astype(jnp.float32))
    chunk_decay = jnp.exp(log_gamma * c)[None, :, None, None]
    u = jnp.einsum('bhnjd,bhnjv->nbhdv', kc * zeta[:, None, :, None], vc)

    def step(state, u_i):
        return state * chunk_decay + u_i, state

    _, prev = lax.scan(step, jnp.zeros((b, h, dk, dv), jnp.float32), u)
    y_cross = jnp.einsum('bhnid,nbhdv->bhniv', qc * xi[:, None, :, None], prev)
    return (y_inner + y_cross).reshape(b, h, s, dv)


def bidirectional_retention(p, decay_logit, gn_g):
    b, s, _ = p.shape
    q, k, v, gate = jnp.split(p, [RET_QK_WIDTH, 2 * RET_QK_WIDTH, 2 * RET_QK_WIDTH + RET_V_WIDTH], axis=-1)

    def heads(t, d):
        return t.reshape(b, s, RET_HEADS, d).transpose(0, 2, 1, 3).astype(jnp.float32)

    q = heads(q, RET_QK_DIM)
    k = heads(k, RET_QK_DIM) * (RET_QK_DIM ** -0.5)
    v = heads(v, RET_V_DIM)
    log_gamma = jax.nn.log_sigmoid(decay_logit.astype(jnp.float32))
    fwd = chunkwise_retention(q, k, v, log_gamma[0], strict=False)
    bwd = jnp.flip(chunkwise_retention(jnp.flip(q, 2), jnp.flip(k, 2), jnp.flip(v, 2), log_gamma[1], strict=True), 2)
    y = fwd + bwd
    mu = jnp.mean(y, axis=-1, keepdims=True)
    var = jnp.mean(jnp.square(y - mu), axis=-1, keepdims=True)
    y = (y - mu) * lax.rsqrt(var + NORM_EPS)
    y = y.transpose(0, 2, 1, 3).reshape(b, s, RET_V_WIDTH) * gn_g.astype(jnp.float32)
    return (jax.nn.silu(gate.astype(jnp.float32)) * y).astype(p.dtype)


def memory_cross_attention(q_p, mem, mem_norm_g, w_mem_kv, q_norm_g, k_norm_g):
    b, s, _ = q_p.shape
    m = mem.shape[1]
    q = rms_norm(q_p.reshape(b, s, MEM_HEADS, MEM_HEAD_DIM), q_norm_g).astype(jnp.float32)
    kv = (rms_norm(mem, mem_norm_g) @ w_mem_kv).reshape(b, m, 2, MEM_HEADS, MEM_HEAD_DIM)
    k = rms_norm(kv[:, :, 0], k_norm_g).astype(jnp.float32)
    v = kv[:, :, 1].astype(jnp.float32)
    scores = jnp.einsum('bshd,bmhd->bhsm', q, k) * (MEM_HEAD_DIM ** -0.5)
    attn = jax.nn.softmax(scores, axis=-1)
    o = jnp.einsum('bhsm,bmhd->bshd', attn, v)
    return o.reshape(b, s, MEM_WIDTH).astype(q_p.dtype)


def conv_glu_ffn(h, w_ffn_in, conv_w, conv_b, w_ffn_out):
    u, gate = jnp.split(h @ w_ffn_in, 2, axis=-1)
    s = u.shape[1]
    pad = CONV_WIDTH // 2
    up = jnp.pad(u, ((0, 0), (pad, pad), (0, 0)))
    c = conv_b
    for i in range(CONV_WIDTH):
        c = c + up[:, i:i + s] * conv_w[i]
    y = jax.nn.gelu(c, approximate=False) * gate
    return y @ w_ffn_out


def setup_inputs(seed: int = 0) -> dict:
    key = jax.random.key(seed)
    ks = jax.random.split(key, 24)
    f32 = jnp.float32

    def normal(k, shape, scale):
        return jax.random.normal(k, shape, f32) * scale

    def gain(k, shape):
        return 1.0 + 0.02 * jax.random.normal(k, shape, f32)

    base_logit = jnp.log(jnp.exp2(5.0 + jnp.arange(RET_HEADS, dtype=f32)) - 1.0)
    ret_decay_logit = base_logit[None, None, :] + 0.1 * jax.random.normal(ks[6], (DEPTH, 2, RET_HEADS), f32)
    return {
        'x': normal(ks[0], (BATCH, SEQ, D_MODEL), 1.0),
        'mem': normal(ks[1], (BATCH, N_MEM, D_MODEL), 1.0),
        'norm1_g': gain(ks[2], (DEPTH, D_MODEL)),
        'w_in': normal(ks[3], (DEPTH, D_MODEL, IN_COLS), D_MODEL ** -0.5),
        'dil_q_norm_g': gain(ks[4], (DEPTH, N_DIL_GROUPS, DIL_HEAD_DIM)),
        'dil_k_norm_g': gain(ks[5], (DEPTH, N_DIL_GROUPS, DIL_HEAD_DIM)),
        'ret_decay_logit': ret_decay_logit,
        'ret_gn_g': gain(ks[7], (DEPTH, RET_V_WIDTH)),
        'mem_norm_g': gain(ks[8], (DEPTH, D_MODEL)),
        'w_mem_kv': normal(ks[9], (DEPTH, D_MODEL, 2 * MEM_WIDTH), D_MODEL ** -0.5),
        'mem_q_norm_g': gain(ks[10], (DEPTH, MEM_HEAD_DIM)),
        'mem_k_norm_g': gain(ks[11], (DEPTH, MEM_HEAD_DIM)),
        'w_branch_dil': normal(ks[12], (DEPTH, DIL_WIDTH, D_MODEL), DIL_WIDTH ** -0.5),
        'w_branch_ret': normal(ks[13], (DEPTH, RET_V_WIDTH, D_MODEL), RET_V_WIDTH ** -0.5),
        'w_branch_mem': normal(ks[14], (DEPTH, MEM_WIDTH, D_MODEL), MEM_WIDTH ** -0.5),
        'w_out': normal(ks[15], (DEPTH, D_MODEL, D_MODEL), D_MODEL ** -0.5),
        'norm2_g': gain(ks[16], (DEPTH, D_MODEL)),
        'w_ffn_in': normal(ks[17], (DEPTH, D_MODEL, 2 * D_FF), D_MODEL ** -0.5),
        'ffn_conv_w': normal(ks[18], (DEPTH, CONV_WIDTH, D_FF), CONV_WIDTH ** -0.5),
        'ffn_conv_b': normal(ks[19], (DEPTH, D_FF), 0.01),
        'w_ffn_out': normal(ks[20], (DEPTH, D_FF, D_MODEL), D_FF ** -0.5),
    }


def reference(x, mem, norm1_g, w_in, dil_q_norm_g, dil_k_norm_g, ret_decay_logit, ret_gn_g,
              mem_norm_g, w_mem_kv, mem_q_norm_g, mem_k_norm_g, w_branch_dil, w_branch_ret,
              w_branch_mem, w_out, norm2_g, w_ffn_in, ffn_conv_w, ffn_conv_b, w_ffn_out):
    b, s, d = x.shape
    dt = x.dtype
    slopes = alibi_slopes(DIL_HEADS)
    split_at = [DIL_COLS, DIL_COLS + RET_COLS, DIL_COLS + RET_COLS + MEM_WIDTH]
    for l in range(DEPTH):
        h = rms_norm(x, norm1_g[l])
        proj = h @ w_in[l]
        dil_p, ret_p, memq_p, gate_p = jnp.split(proj, split_at, axis=-1)
        y_dil = dilated_mixture_attention(dil_p, dil_q_norm_g[l], dil_k_norm_g[l], slopes)
        y_ret = bidirectional_retention(ret_p, ret_decay_logit[l], ret_gn_g[l])
        y_mem = memory_cross_attention(memq_p, mem, mem_norm_g[l], w_mem_kv[l], mem_q_norm_g[l], mem_k_norm_g[l])
        gates = jax.nn.sigmoid(gate_p.astype(jnp.float32)).reshape(b, s, N_BRANCHES, d)
        merged = (gates[:, :, 0] * (y_dil @ w_branch_dil[l]).astype(jnp.float32)
                  + gates[:, :, 1] * (y_ret @ w_branch_ret[l]).astype(jnp.float32)
                  + gates[:, :, 2] * (y_mem @ w_branch_mem[l]).astype(jnp.float32))
        x = x + merged.astype(dt) @ w_out[l]
        h2 = rms_norm(x, norm2_g[l])
        x = x + conv_glu_ffn(h2, w_ffn_in[l], ffn_conv_w[l], ffn_conv_b[l], w_ffn_out[l]).astype(dt)
    return x
```

```python
import functools

import numpy as np
import jax
import jax.numpy as jnp
from jax import lax
from jax.experimental import pallas as pl
from jax.experimental.pallas import tpu as pltpu

F32 = jnp.float32
BF16 = jnp.bfloat16

D_MODEL = 1024
SEQ = 2048
N_MEM = 256
DIL_PATTERNS = ((128, 1), (512, 4), (2048, 16))
DIL_HEADS = 8
DIL_HEAD_DIM = 64
DIL_WIDTH = DIL_HEADS * DIL_HEAD_DIM
DIL_HALF = 64
RET_HEADS = 4
RET_QK_DIM = 64
RET_V_DIM = 128
RET_CHUNK = 128
MEM_HEADS = 4
MEM_HEAD_DIM = 128
MEM_WIDTH = MEM_HEADS * MEM_HEAD_DIM
D_FF = 2816
NORM_EPS = 1e-6
MASK_VALUE = -1e30
DIL_COLS = 3 * 3 * DIL_WIDTH
RET_COLS = 2 * RET_HEADS * RET_QK_DIM + 2 * RET_HEADS * RET_V_DIM

LANES = 128
MXU_DIM = 256
VMEM_LIMIT_BYTES = 56 * 1024 * 1024

TN = 512
EPI_SCALE, EPI_NORM64, EPI_NORM128, EPI_SILU = 0, 1, 2, 3
N_TILES = 14
TILE_RET_Q, TILE_RET_K, TILE_RET_V, TILE_RET_G, TILE_MEMQ = 9, 10, 11, 12, 13
LBLK = TN // LANES


def _rms(xf, g):
    ms = jnp.mean(xf * xf, axis=-1, keepdims=True)
    return xf * lax.rsqrt(ms + NORM_EPS) * g


def _inproj_kernel(kind_ref, hsel_ref, x_ref, g1_ref, w_ref, gv_ref, bm_ref, o_ref, hs_ref, hn_ref):
    n = pl.program_id(1)
    seq, d = x_ref.shape
    n_slabs = d // LANES

    @pl.when(n == 0)
    def _prologue():
        g = g1_ref[...]
        rc = 256

        def nat(i, c):
            rows = pl.ds(pl.multiple_of(i * rc, rc), rc)
            hn = _rms(x_ref[rows, :], g)
            hs_ref[0, rows, :] = hn.astype(BF16)
            for j in range(n_slabs):
                hn_ref[j, rows, :] = hn[:, j * LANES:(j + 1) * LANES]
            return c

        lax.fori_loop(0, seq // rc, nat, 0)

        for slot, (_, r) in enumerate(DIL_PATTERNS):
            if r == 1:
                continue
            n_sub = seq // r
            rows = min(rc, n_sub)
            per = n_sub // rows

            def perm(i, c, r=r, n_sub=n_sub, rows=rows, per=per, slot=slot):
                cls = i // per
                j = i % per
                src = pl.ds(cls + r * rows * j, rows, stride=r)
                dst = pl.ds(pl.multiple_of(cls * n_sub + j * rows, rows), rows)
                hs_ref[slot, dst, :] = jnp.concatenate(
                    [hn_ref[k, src, :] for k in range(n_slabs)], axis=1).astype(BF16)
                return c

            lax.fori_loop(0, r * per, perm, 0)

    kind = kind_ref[n]
    slot = hsel_ref[n]
    gv = gv_ref[...]
    rm = 512

    def run(epilogue):
        def body(i, c):
            r0 = pl.multiple_of(i * rm, rm)
            acc = jnp.dot(hs_ref[slot, pl.ds(r0, rm), :], w_ref[...], preferred_element_type=F32)
            o_ref[pl.ds(r0, rm), :] = epilogue(acc).astype(BF16)
            return c

        lax.fori_loop(0, seq // rm, body, 0)

    def head_norm(acc, which, head_dim):
        sq = (acc * acc).astype(BF16)
        ms = jnp.concatenate(
            [jnp.dot(sq[:, c0:c0 + MXU_DIM], bm_ref[which], preferred_element_type=F32)
             for c0 in range(0, TN, MXU_DIM)], axis=1)
        return acc * lax.rsqrt(ms * (1.0 / head_dim) + NORM_EPS) * gv

    @pl.when(kind == EPI_SCALE)
    def _():
        run(lambda acc: acc * gv)

    @pl.when(kind == EPI_NORM64)
    def _():
        run(lambda acc: head_norm(acc, 0, DIL_HEAD_DIM))

    @pl.when(kind == EPI_NORM128)
    def _():
        run(lambda acc: head_norm(acc, 1, MEM_HEAD_DIM))

    @pl.when(kind == EPI_SILU)
    def _():
        run(lambda acc: acc * jax.nn.sigmoid(acc))


def _block_diag_ones(block):
    i = np.arange(MXU_DIM)
    return (i[:, None] // block == i[None, :] // block).astype(np.float32)


def _in_projection(x, norm1_g, w_in, dil_q_g, dil_k_g, mem_q_g):
    b, s, d = x.shape
    o = DIL_COLS
    wq = w_in[:, o:o + 256].reshape(d, RET_HEADS, 1, RET_QK_DIM)
    wk = w_in[:, o + 256:o + 512].reshape(d, RET_HEADS, 1, RET_QK_DIM)
    dup = lambda w: jnp.broadcast_to(w, (d, RET_HEADS, 2, RET_QK_DIM)).reshape(d, TN)
    w_all = jnp.concatenate(
        [w_in[:, :DIL_COLS], dup(wq), dup(wk), w_in[:, o + 512:o + 1536],
         w_in[:, o + RET_COLS:o + RET_COLS + MEM_WIDTH]], axis=1).astype(BF16)

    ones = jnp.ones((TN,), F32)
    gv, kinds, hsel = [], [], []
    for g in range(3):
        gv += [jnp.tile(dil_q_g[g], DIL_HEADS) * (DIL_HEAD_DIM ** -0.5), jnp.tile(dil_k_g[g], DIL_HEADS), ones]
        kinds += [EPI_NORM64, EPI_NORM64, EPI_SCALE]
        hsel += [g, g, g]
    gv += [ones, ones * (RET_QK_DIM ** -0.5), ones, ones, jnp.tile(mem_q_g, MEM_HEADS) * (MEM_HEAD_DIM ** -0.5)]
    kinds += [EPI_SCALE, EPI_SCALE, EPI_SCALE, EPI_SILU, EPI_NORM128]
    hsel += [0, 0, 0, 0, 0]
    gv = jnp.stack(gv).reshape(N_TILES, 1, TN).astype(F32)
    bm = jnp.asarray(np.stack([_block_diag_ones(DIL_HEAD_DIM), _block_diag_ones(MEM_HEAD_DIM)]), BF16)

    grid_spec = pltpu.PrefetchScalarGridSpec(
        num_scalar_prefetch=2,
        grid=(b, N_TILES),
        in_specs=[
            pl.BlockSpec((None, s, d), lambda i, n, *_: (i, 0, 0)),
            pl.BlockSpec((1, d), lambda i, n, *_: (0, 0)),
            pl.BlockSpec((d, TN), lambda i, n, *_: (0, n)),
            pl.BlockSpec((None, 1, TN), lambda i, n, *_: (n, 0, 0)),
            pl.BlockSpec((2, MXU_DIM, MXU_DIM), lambda i, n, *_: (0, 0, 0)),
        ],
        out_specs=pl.BlockSpec((None, s, TN), lambda i, n, *_: (i, 0, n)),
        scratch_shapes=[pltpu.VMEM((3, s, d), BF16), pltpu.VMEM((d // LANES, s, LANES), F32)],
    )
    return pl.pallas_call(
        _inproj_kernel,
        grid_spec=grid_spec,
        out_shape=jax.ShapeDtypeStruct((b, s, N_TILES * TN), BF16),
        compiler_params=pltpu.CompilerParams(
            dimension_semantics=("arbitrary", "arbitrary"), vmem_limit_bytes=VMEM_LIMIT_BYTES),
        name="in_projection",
    )(jnp.asarray(kinds, jnp.int32), jnp.asarray(hsel, jnp.int32), x, norm1_g.reshape(1, d), w_all, gv, bm)


QB = 128
KW = 256


def _dil_kernel(slope_ref, *refs):
    qkv = refs[:9]
    o_ref = refs[9]
    acc_s, m_s, d_s, bias_s = refs[10:]
    hp = pl.program_id(1)
    seq = o_ref.shape[0]
    s0 = slope_ref[2 * hp]
    s1 = slope_ref[2 * hp + 1]

    lane = lax.broadcasted_iota(jnp.int32, (QB, LANES), 1)
    first = lane < DIL_HEAD_DIM

    for g, (_, r) in enumerate(DIL_PATTERNS):
        q_ref, k_ref, v_ref = qkv[3 * g:3 * g + 3]
        n_sub = seq // r
        per = n_sub // QB
        kw = min(KW, n_sub)

        row = lax.broadcasted_iota(jnp.int32, (2 * QB, kw), 0)
        col = lax.broadcasted_iota(jnp.int32, (2 * QB, kw), 1)
        base = (col - (row & (QB - 1))).astype(F32)
        slope = jnp.where(row >= QB, s1, s0) * float(r)
        offsets = (0, -DIL_HALF, -2 * DIL_HALF) if kw == KW else (0,)
        for vi, off in enumerate(offsets):
            a = jnp.abs(base + float(off))
            bias_s[vi, :, :kw] = jnp.where(a <= float(DIL_HALF), -slope * a, MASK_VALUE)

        def block(t, carry, g=g, r=r, n_sub=n_sub, per=per, kw=kw, q_ref=q_ref, k_ref=k_ref, v_ref=v_ref):
            cls = t // per
            qb = t % per
            q_rows = pl.ds(pl.multiple_of(t * QB, QB), QB)
            if kw == KW:
                ks = jnp.clip(qb * QB - DIL_HALF, 0, n_sub - KW)
                kabs = pl.multiple_of(cls * n_sub + ks, DIL_HALF)
                vi = jnp.where(qb == 0, 0, jnp.where(qb == per - 1, 2, 1))
            else:
                kabs = pl.multiple_of(t * QB, QB)
                vi = 0
            q = q_ref[q_rows, :]
            kwin = k_ref[pl.ds(kabs, kw), :]
            vwin = v_ref[pl.ds(kabs, kw), :]
            zero = jnp.zeros_like(q)
            q2 = jnp.concatenate([jnp.where(first, q, zero), jnp.where(first, zero, q)], axis=0)
            sc = lax.dot_general(q2, kwin, (((1,), (1,)), ((), ())), preferred_element_type=F32)
            sc = sc + bias_s[vi, :, :kw]
            m = jnp.max(sc, axis=1, keepdims=True)
            p = jnp.exp(sc - m)
            dsum = jnp.sum(p, axis=1, keepdims=True)
            o = jnp.dot(p.astype(BF16), vwin, preferred_element_type=F32)
            accb = jnp.where(first, o[:QB], o[QB:])
            mb = jnp.where(first, m[:QB], m[QB:])
            db = jnp.where(first, dsum[:QB], dsum[QB:])
            if r == 1:
                acc_s[q_rows, :] = accb
                m_s[q_rows, :] = mb
                d_s[q_rows, :] = db
            else:
                st = pl.ds(cls + r * QB * qb, QB, stride=r)
                m_old = m_s[st, :]
                m_new = jnp.maximum(m_old, mb)
                a_old = jnp.exp(m_old - m_new)
                a_blk = jnp.exp(mb - m_new)
                acc_s[st, :] = a_old * acc_s[st, :] + a_blk * accb
                d_s[st, :] = a_old * d_s[st, :] + a_blk * db
                m_s[st, :] = m_new
            return carry

        lax.fori_loop(0, seq // QB, block, 0)

    rc = 256

    def fin(i, c):
        rows = pl.ds(pl.multiple_of(i * rc, rc), rc)
        o_ref[rows, :] = (acc_s[rows, :] / d_s[rows, :]).astype(BF16)
        return c

    lax.fori_loop(0, seq // rc, fin, 0)


def _dilated_attention(proj):
    b, s, _ = proj.shape
    assert DIL_PATTERNS[0][1] == 1 and all(w // (2 * r) == DIL_HALF for w, r in DIL_PATTERNS)
    slopes = jnp.exp2(-jnp.arange(1, DIL_HEADS + 1, dtype=F32) * (8.0 / DIL_HEADS))
    in_specs = [
        pl.BlockSpec((None, s, LANES), functools.partial(lambda i, hp, *_, t: (i, 0, t * LBLK + hp), t=t))
        for t in range(9)
    ]
    grid_spec = pltpu.PrefetchScalarGridSpec(
        num_scalar_prefetch=1,
        grid=(b, DIL_HEADS // 2),
        in_specs=in_specs,
        out_specs=pl.BlockSpec((None, s, LANES), lambda i, hp, *_: (i, 0, hp)),
        scratch_shapes=[pltpu.VMEM((s, LANES), F32), pltpu.VMEM((s, LANES), F32), pltpu.VMEM((s, LANES), F32),
                        pltpu.VMEM((3, 2 * QB, KW), F32)],
    )
    return pl.pallas_call(
        _dil_kernel,
        grid_spec=grid_spec,
        out_shape=jax.ShapeDtypeStruct((b, s, DIL_WIDTH), BF16),
        compiler_params=pltpu.CompilerParams(
            dimension_semantics=("arbitrary", "arbitrary"), vmem_limit_bytes=VMEM_LIMIT_BYTES),
        name="dilated_attention",
    )(slopes, *([proj] * 9))


def _ret_kernel(dl_ref, qq_ref, kk_ref, v_ref, sg_ref, gn_ref, o_ref, u_s, st_s):
    h = pl.program_id(1)
    seq = o_ref.shape[0]
    c = RET_CHUNK
    n_chunks = seq // c
    dk = RET_QK_DIM

    def log_sigmoid(z):
        return jnp.minimum(z, 0.0) - jnp.log1p(jnp.exp(-jnp.abs(z)))

    lgf = log_sigmoid(jnp.full((1, LANES), dl_ref[h], F32))
    lgb = log_sigmoid(jnp.full((1, LANES), dl_ref[RET_HEADS + h], F32))
    ii = lax.broadcasted_iota(jnp.int32, (c, LANES), 0).astype(F32)
    jj = lax.broadcasted_iota(jnp.int32, (c, LANES), 1).astype(F32)
    first = lax.broadcasted_iota(jnp.int32, (c, LANES), 1) < dk
    diff = ii - jj
    decay = jnp.exp(jnp.where(diff >= 0.0, lgf * diff, -lgb * diff))
    xi = jnp.exp(jnp.where(first, lgf * (ii + 1.0), lgb * (float(c) - ii)))
    zeta = jnp.exp(jnp.where(first, lgf * (float(c - 1) - ii), lgb * ii))
    chunk_f = jnp.exp(lgf * float(c))
    chunk_b = jnp.exp(lgb * float(c))

    def rows_of(n):
        return pl.ds(pl.multiple_of(n * c, c), c)

    def incr(n, carry):
        kz = (kk_ref[rows_of(n), :].astype(F32) * zeta).T.astype(BF16)
        u_s[n] = jnp.dot(kz, v_ref[rows_of(n), :], preferred_element_type=F32)
        return carry

    lax.fori_loop(0, n_chunks, incr, 0)

    def scan_f(n, state):
        st_s[n, :dk, :] = state.astype(BF16)
        return state * chunk_f + u_s[n, :dk, :]

    lax.fori_loop(0, n_chunks, scan_f, jnp.zeros((dk, LANES), F32))

    def scan_b(i, state):
        n = n_chunks - 1 - i
        st_s[n, dk:, :] = state.astype(BF16)
        return state * chunk_b + u_s[n, dk:, :]

    lax.fori_loop(0, n_chunks, scan_b, jnp.zeros((dk, LANES), F32))

    gn = gn_ref[...]

    def out(n, carry):
        rows = rows_of(n)
        qq = qq_ref[rows, :]
        kk = kk_ref[rows, :]
        q0 = jnp.where(first, qq, jnp.zeros_like(qq))
        a = lax.dot_general(q0, kk, (((1,), (1,)), ((), ())), preferred_element_type=F32)
        p = (a * decay).astype(BF16)
        qx = (qq.astype(F32) * xi).astype(BF16)
        y = (jnp.dot(p, v_ref[rows, :], preferred_element_type=F32)
             + jnp.dot(qx, st_s[n], preferred_element_type=F32))
        mu = jnp.mean(y, axis=-1, keepdims=True)
        yc = y - mu
        var = jnp.mean(yc * yc, axis=-1, keepdims=True)
        yn = yc * lax.rsqrt(var + NORM_EPS)
        o_ref[rows, :] = (sg_ref[rows, :].astype(F32) * (yn * gn)).astype(BF16)
        return carry

    lax.fori_loop(0, n_chunks, out, 0)


def _retention(proj, decay_logit, gn_g):
    b, s, _ = proj.shape
    n_chunks = s // RET_CHUNK

    def col(tile):
        return lambda i, h, *_: (i, 0, tile * LBLK + h)

    grid_spec = pltpu.PrefetchScalarGridSpec(
        num_scalar_prefetch=1,
        grid=(b, RET_HEADS),
        in_specs=[pl.BlockSpec((None, s, LANES), col(t)) for t in (TILE_RET_Q, TILE_RET_K, TILE_RET_V, TILE_RET_G)]
        + [pl.BlockSpec((1, LANES), lambda i, h, *_: (0, h))],
        out_specs=pl.BlockSpec((None, s, LANES), lambda i, h, *_: (i, 0, h)),
        scratch_shapes=[pltpu.VMEM((n_chunks, 2 * RET_QK_DIM, RET_V_DIM), F32),
                        pltpu.VMEM((n_chunks, 2 * RET_QK_DIM, RET_V_DIM), BF16)],
    )
    return pl.pallas_call(
        _ret_kernel,
        grid_spec=grid_spec,
        out_shape=jax.ShapeDtypeStruct((b, s, RET_HEADS * RET_V_DIM), BF16),
        compiler_params=pltpu.CompilerParams(
            dimension_semantics=("arbitrary", "arbitrary"), vmem_limit_bytes=VMEM_LIMIT_BYTES),
        name="retention",
    )(decay_logit.astype(F32).reshape(-1), proj, proj, proj, proj, gn_g.reshape(1, -1))


def _mem_kernel(mem_ref, mg_ref, wkv_ref, kg_ref, q_ref, o_ref, k_s, v_s):
    seq = o_ref.shape[0]
    hm = _rms(mem_ref[...], mg_ref[...]).astype(BF16)
    kv = jnp.dot(hm, wkv_ref[...], preferred_element_type=F32)
    kg = kg_ref[...]
    for h in range(MEM_HEADS):
        c0 = h * MEM_HEAD_DIM
        k_s[h] = _rms(kv[:, c0:c0 + MEM_HEAD_DIM], kg).astype(BF16)
        v_s[h] = kv[:, MEM_WIDTH + c0:MEM_WIDTH + c0 + MEM_HEAD_DIM].astype(BF16)

    rb = 256

    def block(i, carry):
        rows = pl.ds(pl.multiple_of(i * rb, rb), rb)
        for h in range(MEM_HEADS):
            c0 = h * MEM_HEAD_DIM
            q = q_ref[rows, c0:c0 + MEM_HEAD_DIM]
            sc = lax.dot_general(q, k_s[h], (((1,), (1,)), ((), ())), preferred_element_type=F32)
            m = jnp.max(sc, axis=1, keepdims=True)
            p = jnp.exp(sc - m)
            dsum = jnp.sum(p, axis=1, keepdims=True)
            o = jnp.dot(p.astype(BF16), v_s[h], preferred_element_type=F32)
            o_ref[rows, c0:c0 + MEM_HEAD_DIM] = (o / dsum).astype(BF16)
        return carry

    lax.fori_loop(0, seq // rb, block, 0)


def _memory_attention(proj, mem, mem_norm_g, w_mem_kv, mem_k_g):
    b, s, _ = proj.shape
    m, d = mem.shape[1:]
    return pl.pallas_call(
        _mem_kernel,
        grid=(b,),
        in_specs=[
            pl.BlockSpec((None, m, d), lambda i: (i, 0, 0)),
            pl.BlockSpec((1, d), lambda i: (0, 0)),
            pl.BlockSpec((d, 2 * MEM_WIDTH), lambda i: (0, 0)),
            pl.BlockSpec((1, MEM_HEAD_DIM), lambda i: (0, 0)),
            pl.BlockSpec((None, s, TN), lambda i: (i, 0, TILE_MEMQ)),
        ],
        out_specs=pl.BlockSpec((None, s, MEM_WIDTH), lambda i: (i, 0, 0)),
        out_shape=jax.ShapeDtypeStruct((b, s, MEM_WIDTH), BF16),
        scratch_shapes=[pltpu.VMEM((MEM_HEADS, m, MEM_HEAD_DIM), BF16), pltpu.VMEM((MEM_HEADS, m, MEM_HEAD_DIM), BF16)],
        compiler_params=pltpu.CompilerParams(
            dimension_semantics=("arbitrary",), vmem_limit_bytes=VMEM_LIMIT_BYTES),
        name="memory_attention",
    )(mem, mem_norm_g.reshape(1, d), w_mem_kv.astype(BF16), mem_k_g.reshape(1, -1), proj)


TM_MERGE = 512


def _merge_kernel(x_ref, g1_ref, wg_ref, yd_ref, yr_ref, ym_ref, wd_ref, wr_ref, wm_ref, wo_ref, g2_ref,
                  x1_ref, h2_ref):
    d = x_ref.shape[1]
    x = x_ref[...]
    h = _rms(x, g1_ref[...]).astype(BF16)
    merged = None
    for j, (y_ref, w_ref) in enumerate(((yd_ref, wd_ref), (yr_ref, wr_ref), (ym_ref, wm_ref))):
        gate = jax.nn.sigmoid(jnp.dot(h, wg_ref[:, j * d:(j + 1) * d], preferred_element_type=F32))
        term = gate * jnp.dot(y_ref[...], w_ref[...], preferred_element_type=F32)
        merged = term if merged is None else merged + term
    x1 = x + jnp.dot(merged.astype(BF16), wo_ref[...], preferred_element_type=F32)
    x1_ref[...] = x1
    h2_ref[...] = _rms(x1, g2_ref[...]).astype(BF16)


def _merge_out(x, norm1_g, w_gate, y_dil, y_ret, y_mem, w_bd, w_br, w_bm, w_out, norm2_g):
    b, s, d = x.shape
    t = b * s
    tm = TM_MERGE
    row = lambda w: pl.BlockSpec((tm, w), lambda i: (i, 0))
    full = lambda r, c: pl.BlockSpec((r, c), lambda i: (0, 0))
    x1, h2 = pl.pallas_call(
        _merge_kernel,
        grid=(t // tm,),
        in_specs=[row(d), full(1, d), full(d, 3 * d), row(DIL_WIDTH), row(RET_HEADS * RET_V_DIM), row(MEM_WIDTH),
                  full(DIL_WIDTH, d), full(RET_HEADS * RET_V_DIM, d), full(MEM_WIDTH, d), full(d, d), full(1, d)],
        out_specs=[row(d), row(d)],
        out_shape=[jax.ShapeDtypeStruct((t, d), F32), jax.ShapeDtypeStruct((t, d), BF16)],
        compiler_params=pltpu.CompilerParams(
            dimension_semantics=("arbitrary",), vmem_limit_bytes=VMEM_LIMIT_BYTES),
        name="merge_out_projection",
    )(x.reshape(t, d), norm1_g.reshape(1, d), w_gate.astype(BF16), y_dil.reshape(t, -1), y_ret.reshape(t, -1),
      y_mem.reshape(t, -1), w_bd.astype(BF16), w_br.astype(BF16), w_bm.astype(BF16), w_out.astype(BF16),
      norm2_g.reshape(1, d))
    return x1.reshape(b, s, d), h2.reshape(b, s, d)


TF = 256


def _ffn_kernel(h2_ref, x1_ref, wu_ref, wg_ref, cw_ref, cb_ref, wo_ref, o_ref, acc_s):
    f = pl.program_id(1)
    seq = h2_ref.shape[0]

    @pl.when(f == 0)
    def _():
        acc_s[...] = x1_ref[...]

    hh = h2_ref[...]
    u = jnp.dot(hh, wu_ref[...], preferred_element_type=F32)
    gt = jnp.dot(hh, wg_ref[...], preferred_element_type=F32)
    row = lax.broadcasted_iota(jnp.int32, u.shape, 0)
    u_prev = jnp.where(row == 0, 0.0, pltpu.roll(u, 1, axis=0))
    u_next = jnp.where(row == seq - 1, 0.0, pltpu.roll(u, seq - 1, axis=0))
    cw = cw_ref[...]
    c = cb_ref[...] + u_prev * cw[0:1] + u * cw[1:2] + u_next * cw[2:3]
    y = (0.5 * c * (1.0 + lax.erf(c * (2.0 ** -0.5)))) * gt
    acc_s[...] += jnp.dot(y.astype(BF16), wo_ref[...], preferred_element_type=F32)

    @pl.when(f == pl.num_programs(1) - 1)
    def _():
        o_ref[...] = acc_s[...]


def _ffn(h2, x1, w_ffn_in, conv_w, conv_b, w_ffn_out):
    b, s, d = x1.shape
    nf = D_FF // TF
    w_in = w_ffn_in.astype(BF16)
    return pl.pallas_call(
        _ffn_kernel,
        grid=(b, nf),
        in_specs=[
            pl.BlockSpec((None, s, d), lambda i, f: (i, 0, 0)),
            pl.BlockSpec((None, s, d), lambda i, f: (i, 0, 0)),
            pl.BlockSpec((d, TF), lambda i, f: (0, f)),
            pl.BlockSpec((d, TF), lambda i, f: (0, nf + f)),
            pl.BlockSpec((3, TF), lambda i, f: (0, f)),
            pl.BlockSpec((1, TF), lambda i, f: (0, f)),
            pl.BlockSpec((TF, d), lambda i, f: (f, 0)),
        ],
        out_specs=pl.BlockSpec((None, s, d), lambda i, f: (i, 0, 0)),
        out_shape=jax.ShapeDtypeStruct((b, s, d), F32),
        scratch_shapes=[pltpu.VMEM((s, d), F32)],
        compiler_params=pltpu.CompilerParams(
            dimension_semantics=("arbitrary", "arbitrary"), vmem_limit_bytes=VMEM_LIMIT_BYTES),
        name="conv_glu_ffn",
    )(h2, x1, w_in, w_in, conv_w, conv_b.reshape(1, -1), w_ffn_out.astype(BF16))


def kernel(x, mem, norm1_g, w_in, dil_q_norm_g, dil_k_norm_g, ret_decay_logit, ret_gn_g, mem_norm_g, w_mem_kv,
           mem_q_norm_g, mem_k_norm_g, w_branch_dil, w_branch_ret, w_branch_mem, w_out, norm2_g, w_ffn_in,
           ffn_conv_w, ffn_conv_b, w_ffn_out):
    depth = w_in.shape[0]
    gate0 = DIL_COLS + RET_COLS + MEM_WIDTH
    for l in range(depth):
        proj = _in_projection(x, norm1_g[l], w_in[l], dil_q_norm_g[l], dil_k_norm_g[l], mem_q_norm_g[l])
        y_dil = _dilated_attention(proj)
        y_ret = _retention(proj, ret_decay_logit[l], ret_gn_g[l])
        y_mem = _memory_attention(proj, mem, mem_norm_g[l], w_mem_kv[l], mem_k_norm_g[l])
        x1, h2 = _merge_out(x, norm1_g[l], w_in[l][:, gate0:], y_dil, y_ret, y_mem, w_branch_dil[l],
                            w_branch_ret[l], w_branch_mem[l], w_out[l], norm2_g[l])
        x = _ffn(h2, x1, w_ffn_in[l], ffn_conv_w[l], ffn_conv_b[l], w_ffn_out[l])
    return x
```

```python
import functools

import numpy as np
import jax
import jax.numpy as jnp
from jax import lax
from jax.experimental import pallas as pl
from jax.experimental.pallas import tpu as pltpu

F32 = jnp.float32
BF16 = jnp.bfloat16

D_MODEL = 1024
SEQ = 2048
N_MEM = 256
DIL_PATTERNS = ((128, 1), (512, 4), (2048, 16))
DIL_HEADS = 8
DIL_HEAD_DIM = 64
DIL_WIDTH = DIL_HEADS * DIL_HEAD_DIM
DIL_HALF = 64
RET_HEADS = 4
RET_QK_DIM = 64
RET_V_DIM = 128
RET_CHUNK = 128
MEM_HEADS = 4
MEM_HEAD_DIM = 128
MEM_WIDTH = MEM_HEADS * MEM_HEAD_DIM
D_FF = 2816
NORM_EPS = 1e-6
MASK_VALUE = -1e30
LOG2E = 1.4426950408889634
DIL_COLS = 3 * 3 * DIL_WIDTH
RET_COLS = 2 * RET_HEADS * RET_QK_DIM + 2 * RET_HEADS * RET_V_DIM

LANES = 128
MXU_DIM = 256
VMEM_LIMIT_BYTES = 56 * 1024 * 1024

TN = 512
EPI_SCALE, EPI_NORM64, EPI_NORM128, EPI_SILU = 0, 1, 2, 3
N_TILES = 14
TILE_RET_Q, TILE_RET_K, TILE_RET_V, TILE_RET_G, TILE_MEMQ = 9, 10, 11, 12, 13
LBLK = TN // LANES


def _rms(xf, g):
    ms = jnp.mean(xf * xf, axis=-1, keepdims=True)
    return xf * lax.rsqrt(ms + NORM_EPS) * g


def _inproj_kernel(kind_ref, hsel_ref, x_ref, g1_ref, w_ref, gv_ref, bm_ref, o_ref, hs_ref, hn_ref):
    n = pl.program_id(1)
    seq, d = x_ref.shape
    n_slabs = d // LANES

    @pl.when(n == 0)
    def _prologue():
        g = g1_ref[...]
        rc = 256

        def nat(i, c):
            rows = pl.ds(pl.multiple_of(i * rc, rc), rc)
            hn = _rms(x_ref[rows, :], g)
            hs_ref[0, rows, :] = hn.astype(BF16)
            for j in range(n_slabs):
                hn_ref[j, rows, :] = hn[:, j * LANES:(j + 1) * LANES]
            return c

        lax.fori_loop(0, seq // rc, nat, 0)

        for slot, (_, r) in enumerate(DIL_PATTERNS):
            if r == 1:
                continue
            n_sub = seq // r
            rows = min(rc, n_sub)
            per = n_sub // rows

            def perm(i, c, r=r, n_sub=n_sub, rows=rows, per=per, slot=slot):
                cls = i // per
                j = i % per
                src = pl.ds(cls + r * rows * j, rows, stride=r)
                dst = pl.ds(pl.multiple_of(cls * n_sub + j * rows, rows), rows)
                hs_ref[slot, dst, :] = jnp.concatenate(
                    [hn_ref[k, src, :] for k in range(n_slabs)], axis=1).astype(BF16)
                return c

            lax.fori_loop(0, r * per, perm, 0)

    kind = kind_ref[n]
    slot = hsel_ref[n]
    gv = gv_ref[...]
    rm = 512

    def run(epilogue):
        def body(i, c):
            r0 = pl.multiple_of(i * rm, rm)
            acc = jnp.dot(hs_ref[slot, pl.ds(r0, rm), :], w_ref[...], preferred_element_type=F32)
            o_ref[pl.ds(r0, rm), :] = epilogue(acc).astype(BF16)
            return c

        lax.fori_loop(0, seq // rm, body, 0, unroll=True)

    def head_norm(acc, which, head_dim):
        sq = (acc * acc).astype(BF16)
        ms = jnp.concatenate(
            [jnp.dot(sq[:, c0:c0 + MXU_DIM], bm_ref[which], preferred_element_type=F32)
             for c0 in range(0, TN, MXU_DIM)], axis=1)
        return acc * lax.rsqrt(ms * (1.0 / head_dim) + NORM_EPS) * gv

    @pl.when(kind == EPI_SCALE)
    def _():
        run(lambda acc: acc * gv)

    @pl.when(kind == EPI_NORM64)
    def _():
        run(lambda acc: head_norm(acc, 0, DIL_HEAD_DIM))

    @pl.when(kind == EPI_NORM128)
    def _():
        run(lambda acc: head_norm(acc, 1, MEM_HEAD_DIM))

    @pl.when(kind == EPI_SILU)
    def _():
        run(lambda acc: acc * jax.nn.sigmoid(acc))


def _block_diag_ones(block):
    i = np.arange(MXU_DIM)
    return (i[:, None] // block == i[None, :] // block).astype(np.float32)


def _in_projection(x, norm1_g, w_in, dil_q_g, dil_k_g, mem_q_g):
    b, s, d = x.shape
    o = DIL_COLS
    wq = w_in[:, o:o + 256].reshape(d, RET_HEADS, 1, RET_QK_DIM)
    wk = w_in[:, o + 256:o + 512].reshape(d, RET_HEADS, 1, RET_QK_DIM)
    dup = lambda w: jnp.broadcast_to(w, (d, RET_HEADS, 2, RET_QK_DIM)).reshape(d, TN)
    w_all = jnp.concatenate(
        [w_in[:, :DIL_COLS], dup(wq), dup(wk), w_in[:, o + 512:o + 1536],
         w_in[:, o + RET_COLS:o + RET_COLS + MEM_WIDTH]], axis=1).astype(BF16)

    ones = jnp.ones((TN,), F32)
    gv, kinds, hsel = [], [], []
    for g in range(3):
        gv += [jnp.tile(dil_q_g[g], DIL_HEADS) * (DIL_HEAD_DIM ** -0.5 * LOG2E), jnp.tile(dil_k_g[g], DIL_HEADS),
               ones]
        kinds += [EPI_NORM64, EPI_NORM64, EPI_SCALE]
        hsel += [g, g, g]
    gv += [ones, ones * (RET_QK_DIM ** -0.5), ones, ones, jnp.tile(mem_q_g, MEM_HEADS) * (MEM_HEAD_DIM ** -0.5)]
    kinds += [EPI_SCALE, EPI_SCALE, EPI_SCALE, EPI_SILU, EPI_NORM128]
    hsel += [0, 0, 0, 0, 0]
    gv = jnp.stack(gv).reshape(N_TILES, 1, TN).astype(F32)
    bm = jnp.asarray(np.stack([_block_diag_ones(DIL_HEAD_DIM), _block_diag_ones(MEM_HEAD_DIM)]), BF16)

    grid_spec = pltpu.PrefetchScalarGridSpec(
        num_scalar_prefetch=2,
        grid=(b, N_TILES),
        in_specs=[
            pl.BlockSpec((None, s, d), lambda i, n, *_: (i, 0, 0)),
            pl.BlockSpec((1, d), lambda i, n, *_: (0, 0)),
            pl.BlockSpec((d, TN), lambda i, n, *_: (0, n)),
            pl.BlockSpec((None, 1, TN), lambda i, n, *_: (n, 0, 0)),
            pl.BlockSpec((2, MXU_DIM, MXU_DIM), lambda i, n, *_: (0, 0, 0)),
        ],
        out_specs=pl.BlockSpec((None, s, TN), lambda i, n, *_: (i, 0, n)),
        scratch_shapes=[pltpu.VMEM((3, s, d), BF16), pltpu.VMEM((d // LANES, s, LANES), F32)],
    )
    return pl.pallas_call(
        _inproj_kernel,
        grid_spec=grid_spec,
        out_shape=jax.ShapeDtypeStruct((b, s, N_TILES * TN), BF16),
        compiler_params=pltpu.CompilerParams(
            dimension_semantics=("arbitrary", "arbitrary"), vmem_limit_bytes=VMEM_LIMIT_BYTES),
        name="in_projection",
    )(jnp.asarray(kinds, jnp.int32), jnp.asarray(hsel, jnp.int32), x, norm1_g.reshape(1, d), w_all, gv, bm)


QB = 128
KW = 256


def _dil_kernel(slope_ref, *refs):
    qkv = refs[:9]
    o_ref = refs[9]
    state = (refs[10:13], refs[13:16])
    bias_s = refs[16]
    hp = pl.program_id(1)
    seq = o_ref.shape[0]
    s0 = slope_ref[2 * hp]
    s1 = slope_ref[2 * hp + 1]

    lane = lax.broadcasted_iota(jnp.int32, (QB, LANES), 1)
    first = lane < DIL_HEAD_DIM
    ones = jnp.ones((KW, LANES), BF16)

    for g, (_, r) in enumerate(DIL_PATTERNS):
        q_ref, k_ref, v_ref = qkv[3 * g:3 * g + 3]
        n_sub = seq // r
        per = n_sub // QB
        kw = min(KW, n_sub)
        r_prev = DIL_PATTERNS[g - 1][1] if g else 1
        ratio = r // r_prev
        src = state[(g - 1) % 2]
        dst = state[g % 2]

        row = lax.broadcasted_iota(jnp.int32, (2 * QB, kw), 0)
        col = lax.broadcasted_iota(jnp.int32, (2 * QB, kw), 1)
        base = (col - (row & (QB - 1))).astype(F32)
        slope = jnp.where(row >= QB, s1, s0) * (float(r) * LOG2E)
        offsets = (0, -DIL_HALF, -2 * DIL_HALF) if kw == KW else (0,)
        for vi, off in enumerate(offsets):
            a = jnp.abs(base + float(off))
            bias_s[vi, :, :kw] = jnp.where(a <= float(DIL_HALF), -slope * a, MASK_VALUE)

        def block(t, carry, g=g, r_prev=r_prev, ratio=ratio, n_sub=n_sub, per=per, kw=kw, src=src, dst=dst,
                  q_ref=q_ref, k_ref=k_ref, v_ref=v_ref):
            cls = t // per
            qb = t % per
            rows = pl.ds(pl.multiple_of(t * QB, QB), QB)
            if kw == KW:
                ks = jnp.clip(qb * QB - DIL_HALF, 0, n_sub - KW)
                kabs = pl.multiple_of(cls * n_sub + ks, DIL_HALF)
                vi = jnp.where(qb == 0, 0, jnp.where(qb == per - 1, 2, 1))
            else:
                kabs = pl.multiple_of(t * QB, QB)
                vi = 0
            q = q_ref[rows, :]
            kwin = k_ref[pl.ds(kabs, kw), :]
            vwin = v_ref[pl.ds(kabs, kw), :]
            zero = jnp.zeros_like(q)
            q2 = jnp.concatenate([jnp.where(first, q, zero), jnp.where(first, zero, q)], axis=0)
            sc = lax.dot_general(q2, kwin, (((1,), (1,)), ((), ())), preferred_element_type=F32)
            sc = sc + bias_s[vi, :, :kw]
            m = jnp.max(sc, axis=1, keepdims=True)
            p = jnp.exp2(sc - m).astype(BF16)
            od = jnp.dot(p, jnp.concatenate([vwin, ones[:kw]], axis=1), preferred_element_type=F32)
            o = od[:, :LANES]
            dd = od[:, LANES:]
            accb = jnp.where(first, o[:QB], o[QB:])
            mb = jnp.where(first, m[:QB], m[QB:])
            db = jnp.where(first, dd[:QB], dd[QB:])
            if g == 0:
                dst[0][rows, :] = accb
                dst[1][rows, :] = mb
                dst[2][rows, :] = db
            else:
                n_sub_prev = seq // r_prev
                start = (cls % r_prev) * n_sub_prev + cls // r_prev + ratio * QB * qb
                prev = pl.ds(start, QB, stride=ratio)
                m_old = src[1][prev, :]
                m_new = jnp.maximum(m_old, mb)
                a_old = jnp.exp2(m_old - m_new)
                a_blk = jnp.exp2(mb - m_new)
                dst[0][rows, :] = a_old * src[0][prev, :] + a_blk * accb
                dst[2][rows, :] = a_old * src[2][prev, :] + a_blk * db
                if g + 1 < len(DIL_PATTERNS):
                    dst[1][rows, :] = m_new
            return carry

        lax.fori_loop(0, seq // QB, block, 0, unroll=True)

    last = len(DIL_PATTERNS) - 1
    r = DIL_PATTERNS[last][1]
    per = seq // r // QB
    acc_ref, _, den_ref = state[last % 2]
    nat_ref = state[(last + 1) % 2][0]

    def unpermute(t, c):
        rows = pl.ds(pl.multiple_of(t * QB, QB), QB)
        nat_ref[pl.ds(t // per + r * QB * (t % per), QB, stride=r), :] = acc_ref[rows, :] / den_ref[rows, :]
        return c

    lax.fori_loop(0, seq // QB, unpermute, 0, unroll=2)
    rc = 256

    def fin(i, c):
        rows = pl.ds(pl.multiple_of(i * rc, rc), rc)
        o_ref[rows, :] = nat_ref[rows, :].astype(BF16)
        return c

    lax.fori_loop(0, seq // rc, fin, 0, unroll=2)


def _dilated_attention(proj):
    b, s, _ = proj.shape
    assert DIL_PATTERNS[0][1] == 1 and all(w // (2 * r) == DIL_HALF for w, r in DIL_PATTERNS)
    slopes = jnp.exp2(-jnp.arange(1, DIL_HEADS + 1, dtype=F32) * (8.0 / DIL_HEADS))
    in_specs = [
        pl.BlockSpec((None, s, LANES), functools.partial(lambda i, hp, *_, t: (i, 0, t * LBLK + hp), t=t))
        for t in range(9)
    ]
    grid_spec = pltpu.PrefetchScalarGridSpec(
        num_scalar_prefetch=1,
        grid=(b, DIL_HEADS // 2),
        in_specs=in_specs,
        out_specs=pl.BlockSpec((None, s, LANES), lambda i, hp, *_: (i, 0, hp)),
        scratch_shapes=[pltpu.VMEM((s, LANES), F32)] * 6 + [pltpu.VMEM((3, 2 * QB, KW), F32)],
    )
    return pl.pallas_call(
        _dil_kernel,
        grid_spec=grid_spec,
        out_shape=jax.ShapeDtypeStruct((b, s, DIL_WIDTH), BF16),
        compiler_params=pltpu.CompilerParams(
            dimension_semantics=("arbitrary", "arbitrary"), vmem_limit_bytes=VMEM_LIMIT_BYTES),
        name="dilated_attention",
    )(slopes, *([proj] * 9))


def _ret_kernel(dl_ref, qq_ref, kk_ref, v_ref, sg_ref, gn_ref, o_ref, u_s, st_s):
    h = pl.program_id(1)
    seq = o_ref.shape[0]
    c = RET_CHUNK
    n_chunks = seq // c
    dk = RET_QK_DIM

    def log_sigmoid(z):
        return jnp.minimum(z, 0.0) - jnp.log1p(jnp.exp(-jnp.abs(z)))

    lgf = log_sigmoid(jnp.full((1, LANES), dl_ref[h], F32))
    lgb = log_sigmoid(jnp.full((1, LANES), dl_ref[RET_HEADS + h], F32))
    ii = lax.broadcasted_iota(jnp.int32, (c, LANES), 0).astype(F32)
    jj = lax.broadcasted_iota(jnp.int32, (c, LANES), 1).astype(F32)
    first = lax.broadcasted_iota(jnp.int32, (c, LANES), 1) < dk
    diff = ii - jj
    decay = jnp.exp(jnp.where(diff >= 0.0, lgf * diff, -lgb * diff))
    xi = jnp.exp(jnp.where(first, lgf * (ii + 1.0), lgb * (float(c) - ii)))
    zeta = jnp.exp(jnp.where(first, lgf * (float(c - 1) - ii), lgb * ii))
    chunk_f = jnp.exp(lgf * float(c))
    chunk_b = jnp.exp(lgb * float(c))

    def rows_of(n):
        return pl.ds(pl.multiple_of(n * c, c), c)

    def incr(n, carry):
        kz = (kk_ref[rows_of(n), :].astype(F32) * zeta).T.astype(BF16)
        u_s[n] = jnp.dot(kz, v_ref[rows_of(n), :], preferred_element_type=F32)
        return carry

    lax.fori_loop(0, n_chunks, incr, 0, unroll=True)

    def scan_f(n, state):
        st_s[n, :dk, :] = state.astype(BF16)
        return state * chunk_f + u_s[n, :dk, :]

    lax.fori_loop(0, n_chunks, scan_f, jnp.zeros((dk, LANES), F32))

    def scan_b(i, state):
        n = n_chunks - 1 - i
        st_s[n, dk:, :] = state.astype(BF16)
        return state * chunk_b + u_s[n, dk:, :]

    lax.fori_loop(0, n_chunks, scan_b, jnp.zeros((dk, LANES), F32))

    gn = gn_ref[...]

    def out(n, carry):
        rows = rows_of(n)
        qq = qq_ref[rows, :]
        kk = kk_ref[rows, :]
        q0 = jnp.where(first, qq, jnp.zeros_like(qq))
        a = lax.dot_general(q0, kk, (((1,), (1,)), ((), ())), preferred_element_type=F32)
        p = (a * decay).astype(BF16)
        qx = (qq.astype(F32) * xi).astype(BF16)
        y = (jnp.dot(p, v_ref[rows, :], preferred_element_type=F32)
             + jnp.dot(qx, st_s[n], preferred_element_type=F32))
        mu = jnp.mean(y, axis=-1, keepdims=True)
        yc = y - mu
        var = jnp.mean(yc * yc, axis=-1, keepdims=True)
        yn = yc * lax.rsqrt(var + NORM_EPS)
        o_ref[rows, :] = (sg_ref[rows, :].astype(F32) * (yn * gn)).astype(BF16)
        return carry

    lax.fori_loop(0, n_chunks, out, 0, unroll=True)


def _retention(proj, decay_logit, gn_g):
    b, s, _ = proj.shape
    n_chunks = s // RET_CHUNK

    def col(tile):
        return lambda i, h, *_: (i, 0, tile * LBLK + h)

    grid_spec = pltpu.PrefetchScalarGridSpec(
        num_scalar_prefetch=1,
        grid=(b, RET_HEADS),
        in_specs=[pl.BlockSpec((None, s, LANES), col(t)) for t in (TILE_RET_Q, TILE_RET_K, TILE_RET_V, TILE_RET_G)]
        + [pl.BlockSpec((1, LANES), lambda i, h, *_: (0, h))],
        out_specs=pl.BlockSpec((None, s, LANES), lambda i, h, *_: (i, 0, h)),
        scratch_shapes=[pltpu.VMEM((n_chunks, 2 * RET_QK_DIM, RET_V_DIM), F32),
                        pltpu.VMEM((n_chunks, 2 * RET_QK_DIM, RET_V_DIM), BF16)],
    )
    return pl.pallas_call(
        _ret_kernel,
        grid_spec=grid_spec,
        out_shape=jax.ShapeDtypeStruct((b, s, RET_HEADS * RET_V_DIM), BF16),
        compiler_params=pltpu.CompilerParams(
            dimension_semantics=("arbitrary", "arbitrary"), vmem_limit_bytes=VMEM_LIMIT_BYTES),
        name="retention",
    )(decay_logit.astype(F32).reshape(-1), proj, proj, proj, proj, gn_g.reshape(1, -1))


def _mem_kernel(mem_ref, mg_ref, wkv_ref, kg_ref, q_ref, o_ref, k_s, v_s):
    seq = o_ref.shape[0]
    hm = _rms(mem_ref[...], mg_ref[...]).astype(BF16)
    kv = jnp.dot(hm, wkv_ref[...], preferred_element_type=F32)
    kg = kg_ref[...]
    for h in range(MEM_HEADS):
        c0 = h * MEM_HEAD_DIM
        k_s[h] = _rms(kv[:, c0:c0 + MEM_HEAD_DIM], kg).astype(BF16)
        v_s[h] = kv[:, MEM_WIDTH + c0:MEM_WIDTH + c0 + MEM_HEAD_DIM].astype(BF16)

    rb = 256

    def block(i, carry):
        rows = pl.ds(pl.multiple_of(i * rb, rb), rb)
        for h in range(MEM_HEADS):
            c0 = h * MEM_HEAD_DIM
            q = q_ref[rows, c0:c0 + MEM_HEAD_DIM]
            sc = lax.dot_general(q, k_s[h], (((1,), (1,)), ((), ())), preferred_element_type=F32)
            m = jnp.max(sc, axis=1, keepdims=True)
            p = jnp.exp(sc - m)
            dsum = jnp.sum(p, axis=1, keepdims=True)
            o = jnp.dot(p.astype(BF16), v_s[h], preferred_element_type=F32)
            o_ref[rows, c0:c0 + MEM_HEAD_DIM] = (o / dsum).astype(BF16)
        return carry

    lax.fori_loop(0, seq // rb, block, 0, unroll=2)


def _memory_attention(proj, mem, mem_norm_g, w_mem_kv, mem_k_g):
    b, s, _ = proj.shape
    m, d = mem.shape[1:]
    return pl.pallas_call(
        _mem_kernel,
        grid=(b,),
        in_specs=[
            pl.BlockSpec((None, m, d), lambda i: (i, 0, 0)),
            pl.BlockSpec((1, d), lambda i: (0, 0)),
            pl.BlockSpec((d, 2 * MEM_WIDTH), lambda i: (0, 0)),
            pl.BlockSpec((1, MEM_HEAD_DIM), lambda i: (0, 0)),
            pl.BlockSpec((None, s, TN), lambda i: (i, 0, TILE_MEMQ)),
        ],
        out_specs=pl.BlockSpec((None, s, MEM_WIDTH), lambda i: (i, 0, 0)),
        out_shape=jax.ShapeDtypeStruct((b, s, MEM_WIDTH), BF16),
        scratch_shapes=[pltpu.VMEM((MEM_HEADS, m, MEM_HEAD_DIM), BF16), pltpu.VMEM((MEM_HEADS, m, MEM_HEAD_DIM), BF16)],
        compiler_params=pltpu.CompilerParams(
            dimension_semantics=("arbitrary",), vmem_limit_bytes=VMEM_LIMIT_BYTES),
        name="memory_attention",
    )(mem, mem_norm_g.reshape(1, d), w_mem_kv.astype(BF16), mem_k_g.reshape(1, -1), proj)


TM_MERGE = 512


def _merge_kernel(x_ref, g1_ref, wg_ref, yd_ref, yr_ref, ym_ref, wd_ref, wr_ref, wm_ref, wo_ref, g2_ref,
                  x1_ref, h2_ref):
    d = x_ref.shape[1]
    x = x_ref[...]
    h = _rms(x, g1_ref[...]).astype(BF16)
    merged = None
    for j, (y_ref, w_ref) in enumerate(((yd_ref, wd_ref), (yr_ref, wr_ref), (ym_ref, wm_ref))):
        gate = jax.nn.sigmoid(jnp.dot(h, wg_ref[:, j * d:(j + 1) * d], preferred_element_type=F32))
        term = gate * jnp.dot(y_ref[...], w_ref[...], preferred_element_type=F32)
        merged = term if merged is None else merged + term
    x1 = x + jnp.dot(merged.astype(BF16), wo_ref[...], preferred_element_type=F32)
    x1_ref[...] = x1
    h2_ref[...] = _rms(x1, g2_ref[...]).astype(BF16)


def _merge_out(x, norm1_g, w_gate, y_dil, y_ret, y_mem, w_bd, w_br, w_bm, w_out, norm2_g):
    b, s, d = x.shape
    t = b * s
    tm = TM_MERGE
    row = lambda w: pl.BlockSpec((tm, w), lambda i: (i, 0))
    full = lambda r, c: pl.BlockSpec((r, c), lambda i: (0, 0))
    x1, h2 = pl.pallas_call(
        _merge_kernel,
        grid=(t // tm,),
        in_specs=[row(d), full(1, d), full(d, 3 * d), row(DIL_WIDTH), row(RET_HEADS * RET_V_DIM), row(MEM_WIDTH),
                  full(DIL_WIDTH, d), full(RET_HEADS * RET_V_DIM, d), full(MEM_WIDTH, d), full(d, d), full(1, d)],
        out_specs=[row(d), row(d)],
        out_shape=[jax.ShapeDtypeStruct((t, d), F32), jax.ShapeDtypeStruct((t, d), BF16)],
        compiler_params=pltpu.CompilerParams(
            dimension_semantics=("arbitrary",), vmem_limit_bytes=VMEM_LIMIT_BYTES),
        name="merge_out_projection",
    )(x.reshape(t, d), norm1_g.reshape(1, d), w_gate.astype(BF16), y_dil.reshape(t, -1), y_ret.reshape(t, -1),
      y_mem.reshape(t, -1), w_bd.astype(BF16), w_br.astype(BF16), w_bm.astype(BF16), w_out.astype(BF16),
      norm2_g.reshape(1, d))
    return x1.reshape(b, s, d), h2.reshape(b, s, d)


TF = 256


def _ffn_kernel(h2_ref, x1_ref, wu_ref, wg_ref, cw_ref, cb_ref, wo_ref, o_ref, acc_s):
    f = pl.program_id(1)
    seq = h2_ref.shape[0]

    @pl.when(f == 0)
    def _():
        acc_s[...] = x1_ref[...]

    hh = h2_ref[...]
    u = jnp.dot(hh, wu_ref[...], preferred_element_type=F32)
    gt = jnp.dot(hh, wg_ref[...], preferred_element_type=F32)
    row = lax.broadcasted_iota(jnp.int32, u.shape, 0)
    u_prev = jnp.where(row == 0, 0.0, pltpu.roll(u, 1, axis=0))
    u_next = jnp.where(row == seq - 1, 0.0, pltpu.roll(u, seq - 1, axis=0))
    cw = cw_ref[...]
    c = cb_ref[...] + u_prev * cw[0:1] + u * cw[1:2] + u_next * cw[2:3]
    y = (0.5 * c * (1.0 + lax.erf(c * (2.0 ** -0.5)))) * gt
    acc_s[...] += jnp.dot(y.astype(BF16), wo_ref[...], preferred_element_type=F32)

    @pl.when(f == pl.num_programs(1) - 1)
    def _():
        o_ref[...] = acc_s[...]


def _ffn(h2, x1, w_ffn_in, conv_w, conv_b, w_ffn_out):
    b, s, d = x1.shape
    nf = D_FF // TF
    w_in = w_ffn_in.astype(BF16)
    return pl.pallas_call(
        _ffn_kernel,
        grid=(b, nf),
        in_specs=[
            pl.BlockSpec((None, s, d), lambda i, f: (i, 0, 0)),
            pl.BlockSpec((None, s, d), lambda i, f: (i, 0, 0)),
            pl.BlockSpec((d, TF), lambda i, f: (0, f)),
            pl.BlockSpec((d, TF), lambda i, f: (0, nf + f)),
            pl.BlockSpec((3, TF), lambda i, f: (0, f)),
            pl.BlockSpec((1, TF), lambda i, f: (0, f)),
            pl.BlockSpec((TF, d), lambda i, f: (f, 0)),
        ],
        out_specs=pl.BlockSpec((None, s, d), lambda i, f: (i, 0, 0)),
        out_shape=jax.ShapeDtypeStruct((b, s, d), F32),
        scratch_shapes=[pltpu.VMEM((s, d), F32)],
        compiler_params=pltpu.CompilerParams(
            dimension_semantics=("arbitrary", "arbitrary"), vmem_limit_bytes=VMEM_LIMIT_BYTES),
        name="conv_glu_ffn",
    )(h2, x1, w_in, w_in, conv_w, conv_b.reshape(1, -1), w_ffn_out.astype(BF16))


def kernel(x, mem, norm1_g, w_in, dil_q_norm_g, dil_k_norm_g, ret_decay_logit, ret_gn_g, mem_norm_g, w_mem_kv,
           mem_q_norm_g, mem_k_norm_g, w_branch_dil, w_branch_ret, w_branch_mem, w_out, norm2_g, w_ffn_in,
           ffn_conv_w, ffn_conv_b, w_ffn_out):
    depth = w_in.shape[0]
    gate0 = DIL_COLS + RET_COLS + MEM_WIDTH
    for l in range(depth):
        proj = _in_projection(x, norm1_g[l], w_in[l], dil_q_norm_g[l], dil_k_norm_g[l], mem_q_norm_g[l])
        y_dil = _dilated_attention(proj)
        y_ret = _retention(proj, ret_decay_logit[l], ret_gn_g[l])
        y_mem = _memory_attention(proj, mem, mem_norm_g[l], w_mem_kv[l], mem_k_norm_g[l])
        x1, h2 = _merge_out(x, norm1_g[l], w_in[l][:, gate0:], y_dil, y_ret, y_mem, w_branch_dil[l],
                            w_branch_ret[l], w_branch_mem[l], w_out[l], norm2_g[l])
        x = _ffn(h2, x1, w_ffn_in[l], ffn_conv_w[l], ffn_conv_b[l], w_ffn_out[l])
    return x
```

```python
import functools

import numpy as np
import jax
import jax.numpy as jnp
from jax import lax
from jax.experimental import pallas as pl
from jax.experimental.pallas import tpu as pltpu

F32 = jnp.float32
BF16 = jnp.bfloat16

D_MODEL = 1024
SEQ = 2048
N_MEM = 256
DIL_PATTERNS = ((128, 1), (512, 4), (2048, 16))
DIL_HEADS = 8
DIL_HEAD_DIM = 64
DIL_WIDTH = DIL_HEADS * DIL_HEAD_DIM
DIL_HALF = 64
RET_HEADS = 4
RET_QK_DIM = 64
RET_V_DIM = 128
RET_CHUNK = 128
MEM_HEADS = 4
MEM_HEAD_DIM = 128
MEM_WIDTH = MEM_HEADS * MEM_HEAD_DIM
D_FF = 2816
NORM_EPS = 1e-6
MASK_VALUE = -1e30
LOG2E = 1.4426950408889634
DIL_COLS = 3 * 3 * DIL_WIDTH
RET_COLS = 2 * RET_HEADS * RET_QK_DIM + 2 * RET_HEADS * RET_V_DIM

LANES = 128
MXU_DIM = 256
VMEM_LIMIT_BYTES = 56 * 1024 * 1024

TN = 512
SUB = 2
EPI_SCALE, EPI_NORM64, EPI_NORM128, EPI_SILU = 0, 1, 2, 3
N_TILES = 14
TILE_RET_Q, TILE_RET_K, TILE_RET_V, TILE_RET_G, TILE_MEMQ = 9, 10, 11, 12, 13
TILE_EPILOGUE = (EPI_NORM64, EPI_NORM64, EPI_SCALE) * 3 + (EPI_SCALE,) * 3 + (EPI_SILU, EPI_NORM128)
TILE_SLOT = (0, 0, 0, 1, 1, 1, 2, 2, 2) + (0,) * 5
LBLK = TN // LANES


def _rms(xf, g):
    ms = jnp.mean(xf * xf, axis=-1, keepdims=True)
    return xf * lax.rsqrt(ms + NORM_EPS) * g


def _inproj_kernel(kind_ref, slot_ref, x_ref, g1_ref, w_ref, gv_ref, bm_ref, o_ref, hs_ref, hn_ref, *, kinds):
    n = pl.program_id(1)
    seq, d = x_ref.shape
    n_slabs = d // LANES

    @pl.when(n == 0)
    def _prologue():
        g = g1_ref[...]
        rc = 256

        def nat(i, c):
            rows = pl.ds(pl.multiple_of(i * rc, rc), rc)
            hn = _rms(x_ref[rows, :], g)
            hs_ref[0, rows, :] = hn.astype(BF16)
            for j in range(n_slabs):
                hn_ref[j, rows, :] = hn[:, j * LANES:(j + 1) * LANES]
            return c

        lax.fori_loop(0, seq // rc, nat, 0)

        for slot, (_, r) in enumerate(DIL_PATTERNS):
            if r == 1:
                continue
            n_sub = seq // r
            rows = min(rc, n_sub)
            per = n_sub // rows

            def perm(i, c, r=r, n_sub=n_sub, rows=rows, per=per, slot=slot):
                cls = i // per
                j = i % per
                src = pl.ds(cls + r * rows * j, rows, stride=r)
                dst = pl.ds(pl.multiple_of(cls * n_sub + j * rows, rows), rows)
                hs_ref[slot, dst, :] = jnp.concatenate(
                    [hn_ref[k, src, :] for k in range(n_slabs)], axis=1).astype(BF16)
                return c

            lax.fori_loop(0, r * per, perm, 0)

    def head_norm(acc, gv, which, head_dim):
        sq = (acc * acc).astype(BF16)
        ms = jnp.concatenate(
            [jnp.dot(sq[:, c0:c0 + MXU_DIM], bm_ref[which], preferred_element_type=F32)
             for c0 in range(0, TN, MXU_DIM)], axis=1)
        return acc * lax.rsqrt(ms * (1.0 / head_dim) + NORM_EPS) * gv

    epilogues = {
        EPI_SCALE: lambda acc, gv: acc * gv,
        EPI_NORM64: lambda acc, gv: head_norm(acc, gv, 0, DIL_HEAD_DIM),
        EPI_NORM128: lambda acc, gv: head_norm(acc, gv, 1, MEM_HEAD_DIM),
        EPI_SILU: lambda acc, gv: acc * jax.nn.sigmoid(acc),
    }
    rm = 512
    for sub in range(SUB):
        kind = kind_ref[n * SUB + sub]
        slot = slot_ref[n * SUB + sub]
        cols = slice(sub * TN, (sub + 1) * TN)
        for epi in kinds[sub]:

            @pl.when(kind == epi)
            def _(epi=epi, slot=slot, cols=cols, sub=sub):
                gv = gv_ref[sub]
                for r0 in range(0, seq, rm):
                    acc = jnp.dot(hs_ref[slot, r0:r0 + rm, :], w_ref[:, cols], preferred_element_type=F32)
                    o_ref[r0:r0 + rm, cols] = epilogues[epi](acc, gv).astype(BF16)


def _block_diag_ones(block):
    i = np.arange(MXU_DIM)
    return (i[:, None] // block == i[None, :] // block).astype(np.float32)


def _in_projection(x, norm1_g, w_in, dil_q_g, dil_k_g, mem_q_g):
    b, s, d = x.shape
    o = DIL_COLS
    wq = w_in[:, o:o + 256].reshape(d, RET_HEADS, 1, RET_QK_DIM)
    wk = w_in[:, o + 256:o + 512].reshape(d, RET_HEADS, 1, RET_QK_DIM)
    dup = lambda w: jnp.broadcast_to(w, (d, RET_HEADS, 2, RET_QK_DIM)).reshape(d, TN)
    w_all = jnp.concatenate(
        [w_in[:, :DIL_COLS], dup(wq), dup(wk), w_in[:, o + 512:o + 1536],
         w_in[:, o + RET_COLS:o + RET_COLS + MEM_WIDTH]], axis=1).astype(BF16)

    ones = jnp.ones((TN,), F32)
    gv = []
    for g in range(3):
        gv += [jnp.tile(dil_q_g[g], DIL_HEADS) * (DIL_HEAD_DIM ** -0.5 * LOG2E), jnp.tile(dil_k_g[g], DIL_HEADS),
               ones]
    gv += [ones, ones * (RET_QK_DIM ** -0.5), ones, ones, jnp.tile(mem_q_g, MEM_HEADS) * (MEM_HEAD_DIM ** -0.5)]
    gv = jnp.stack(gv).reshape(N_TILES, 1, TN).astype(F32)
    bm = jnp.asarray(np.stack([_block_diag_ones(DIL_HEAD_DIM), _block_diag_ones(MEM_HEAD_DIM)]), BF16)
    kinds = tuple(tuple(sorted(set(TILE_EPILOGUE[sub::SUB]))) for sub in range(SUB))

    grid_spec = pltpu.PrefetchScalarGridSpec(
        num_scalar_prefetch=2,
        grid=(b, N_TILES // SUB),
        in_specs=[
            pl.BlockSpec((None, s, d), lambda i, n, *_: (i, 0, 0)),
            pl.BlockSpec((1, d), lambda i, n, *_: (0, 0)),
            pl.BlockSpec((d, SUB * TN), lambda i, n, *_: (0, n)),
            pl.BlockSpec((SUB, 1, TN), lambda i, n, *_: (n, 0, 0)),
            pl.BlockSpec((2, MXU_DIM, MXU_DIM), lambda i, n, *_: (0, 0, 0)),
        ],
        out_specs=pl.BlockSpec((None, s, SUB * TN), lambda i, n, *_: (i, 0, n)),
        scratch_shapes=[pltpu.VMEM((3, s, d), BF16), pltpu.VMEM((d // LANES, s, LANES), F32)],
    )
    return pl.pallas_call(
        functools.partial(_inproj_kernel, kinds=kinds),
        grid_spec=grid_spec,
        out_shape=jax.ShapeDtypeStruct((b, s, N_TILES * TN), BF16),
        compiler_params=pltpu.CompilerParams(
            dimension_semantics=("arbitrary", "arbitrary"), vmem_limit_bytes=VMEM_LIMIT_BYTES),
        name="in_projection",
    )(jnp.asarray(TILE_EPILOGUE, jnp.int32), jnp.asarray(TILE_SLOT, jnp.int32), x, norm1_g.reshape(1, d), w_all,
      gv, bm)


QB = 128
KW = 256


def _dil_kernel(slope_ref, *refs):
    qkv = refs[:9]
    o_ref = refs[9]
    state = (refs[10:13], refs[13:16])
    bias_s = refs[16]
    hp = pl.program_id(1)
    seq = o_ref.shape[0]
    s0 = slope_ref[2 * hp]
    s1 = slope_ref[2 * hp + 1]

    lane = lax.broadcasted_iota(jnp.int32, (QB, LANES), 1)
    first = lane < DIL_HEAD_DIM
    ones = jnp.ones((KW, LANES), BF16)

    for g, (_, r) in enumerate(DIL_PATTERNS):
        q_ref, k_ref, v_ref = qkv[3 * g:3 * g + 3]
        n_sub = seq // r
        per = n_sub // QB
        kw = min(KW, n_sub)
        r_prev = DIL_PATTERNS[g - 1][1] if g else 1
        ratio = r // r_prev
        src = state[(g - 1) % 2]
        dst = state[g % 2]

        row = lax.broadcasted_iota(jnp.int32, (2 * QB, kw), 0)
        col = lax.broadcasted_iota(jnp.int32, (2 * QB, kw), 1)
        base = (col - (row & (QB - 1))).astype(F32)
        slope = jnp.where(row >= QB, s1, s0) * (float(r) * LOG2E)
        offsets = (0, -DIL_HALF, -2 * DIL_HALF) if kw == KW else (0,)
        for vi, off in enumerate(offsets):
            a = jnp.abs(base + float(off))
            bias_s[vi, :, :kw] = jnp.where(a <= float(DIL_HALF), -slope * a, MASK_VALUE)

        def block(t, carry, g=g, r_prev=r_prev, ratio=ratio, n_sub=n_sub, per=per, kw=kw, src=src, dst=dst,
                  q_ref=q_ref, k_ref=k_ref, v_ref=v_ref):
            cls = t // per
            qb = t % per
            rows = pl.ds(pl.multiple_of(t * QB, QB), QB)
            if kw == KW:
                ks = jnp.clip(qb * QB - DIL_HALF, 0, n_sub - KW)
                kabs = pl.multiple_of(cls * n_sub + ks, DIL_HALF)
                vi = jnp.where(qb == 0, 0, jnp.where(qb == per - 1, 2, 1))
            else:
                kabs = pl.multiple_of(t * QB, QB)
                vi = 0
            q = q_ref[rows, :]
            kwin = k_ref[pl.ds(kabs, kw), :]
            vwin = v_ref[pl.ds(kabs, kw), :]
            zero = jnp.zeros_like(q)
            q2 = jnp.concatenate([jnp.where(first, q, zero), jnp.where(first, zero, q)], axis=0)
            sc = lax.dot_general(q2, kwin, (((1,), (1,)), ((), ())), preferred_element_type=F32)
            sc = sc + bias_s[vi, :, :kw]
            m = jnp.max(sc, axis=1, keepdims=True)
            p = jnp.exp2(sc - m).astype(BF16)
            od = jnp.dot(p, jnp.concatenate([vwin, ones[:kw]], axis=1), preferred_element_type=F32)
            o = od[:, :LANES]
            dd = od[:, LANES:]
            accb = jnp.where(first, o[:QB], o[QB:])
            mb = jnp.where(first, m[:QB], m[QB:])
            db = jnp.where(first, dd[:QB], dd[QB:])
            if g == 0:
                dst[0][rows, :] = accb
                dst[1][rows, :] = mb
                dst[2][rows, :] = db
            else:
                n_sub_prev = seq // r_prev
                start = (cls % r_prev) * n_sub_prev + cls // r_prev + ratio * QB * qb
                prev = pl.ds(start, QB, stride=ratio)
                m_old = src[1][prev, :]
                m_new = jnp.maximum(m_old, mb)
                a_old = jnp.exp2(m_old - m_new)
                a_blk = jnp.exp2(mb - m_new)
                dst[0][rows, :] = a_old * src[0][prev, :] + a_blk * accb
                dst[2][rows, :] = a_old * src[2][prev, :] + a_blk * db
                if g + 1 < len(DIL_PATTERNS):
                    dst[1][rows, :] = m_new
            return carry

        lax.fori_loop(0, seq // QB, block, 0, unroll=True)

    last = len(DIL_PATTERNS) - 1
    r = DIL_PATTERNS[last][1]
    per = seq // r // QB
    acc_ref, _, den_ref = state[last % 2]
    nat_ref = state[(last + 1) % 2][0]

    def unpermute(t, c):
        rows = pl.ds(pl.multiple_of(t * QB, QB), QB)
        nat_ref[pl.ds(t // per + r * QB * (t % per), QB, stride=r), :] = acc_ref[rows, :] / den_ref[rows, :]
        return c

    lax.fori_loop(0, seq // QB, unpermute, 0, unroll=2)
    rc = 256

    def fin(i, c):
        rows = pl.ds(pl.multiple_of(i * rc, rc), rc)
        o_ref[rows, :] = nat_ref[rows, :].astype(BF16)
        return c

    lax.fori_loop(0, seq // rc, fin, 0, unroll=2)


def _dilated_attention(proj):
    b, s, _ = proj.shape
    assert DIL_PATTERNS[0][1] == 1 and all(w // (2 * r) == DIL_HALF for w, r in DIL_PATTERNS)
    slopes = jnp.exp2(-jnp.arange(1, DIL_HEADS + 1, dtype=F32) * (8.0 / DIL_HEADS))
    in_specs = [
        pl.BlockSpec((None, s, LANES), functools.partial(lambda i, hp, *_, t: (i, 0, t * LBLK + hp), t=t))
        for t in range(9)
    ]
    grid_spec = pltpu.PrefetchScalarGridSpec(
        num_scalar_prefetch=1,
        grid=(b, DIL_HEADS // 2),
        in_specs=in_specs,
        out_specs=pl.BlockSpec((None, s, LANES), lambda i, hp, *_: (i, 0, hp)),
        scratch_shapes=[pltpu.VMEM((s, LANES), F32)] * 6 + [pltpu.VMEM((3, 2 * QB, KW), F32)],
    )
    return pl.pallas_call(
        _dil_kernel,
        grid_spec=grid_spec,
        out_shape=jax.ShapeDtypeStruct((b, s, DIL_WIDTH), BF16),
        compiler_params=pltpu.CompilerParams(
            dimension_semantics=("arbitrary", "arbitrary"), vmem_limit_bytes=VMEM_LIMIT_BYTES),
        name="dilated_attention",
    )(slopes, *([proj] * 9))


def _ret_kernel(dl_ref, qq_ref, kk_ref, v_ref, sg_ref, gn_ref, o_ref, u_s, st_s):
    h = pl.program_id(1)
    seq = o_ref.shape[0]
    c = RET_CHUNK
    n_chunks = seq // c
    dk = RET_QK_DIM

    def log_sigmoid(z):
        return jnp.minimum(z, 0.0) - jnp.log1p(jnp.exp(-jnp.abs(z)))

    lgf = log_sigmoid(jnp.full((1, LANES), dl_ref[h], F32))
    lgb = log_sigmoid(jnp.full((1, LANES), dl_ref[RET_HEADS + h], F32))
    ii = lax.broadcasted_iota(jnp.int32, (c, LANES), 0).astype(F32)
    jj = lax.broadcasted_iota(jnp.int32, (c, LANES), 1).astype(F32)
    first = lax.broadcasted_iota(jnp.int32, (c, LANES), 1) < dk
    diff = ii - jj
    decay = jnp.exp(jnp.where(diff >= 0.0, lgf * diff, -lgb * diff))
    xi = jnp.exp(jnp.where(first, lgf * (ii + 1.0), lgb * (float(c) - ii)))
    zeta = jnp.exp(jnp.where(first, lgf * (float(c - 1) - ii), lgb * ii))
    chunk_f = jnp.exp(lgf * float(c))
    chunk_b = jnp.exp(lgb * float(c))

    def rows_of(n):
        return pl.ds(pl.multiple_of(n * c, c), c)

    def incr(n, carry):
        kz = (kk_ref[rows_of(n), :].astype(F32) * zeta).T.astype(BF16)
        u_s[n] = jnp.dot(kz, v_ref[rows_of(n), :], preferred_element_type=F32)
        return carry

    lax.fori_loop(0, n_chunks, incr, 0, unroll=True)

    def scan_f(n, state):
        st_s[n, :dk, :] = state.astype(BF16)
        return state * chunk_f + u_s[n, :dk, :]

    lax.fori_loop(0, n_chunks, scan_f, jnp.zeros((dk, LANES), F32))

    def scan_b(i, state):
        n = n_chunks - 1 - i
        st_s[n, dk:, :] = state.astype(BF16)
        return state * chunk_b + u_s[n, dk:, :]

    lax.fori_loop(0, n_chunks, scan_b, jnp.zeros((dk, LANES), F32))

    gn = gn_ref[...]

    def out(n, carry):
        rows = rows_of(n)
        qq = qq_ref[rows, :]
        kk = kk_ref[rows, :]
        q0 = jnp.where(first, qq, jnp.zeros_like(qq))
        a = lax.dot_general(q0, kk, (((1,), (1,)), ((), ())), preferred_element_type=F32)
        p = (a * decay).astype(BF16)
        qx = (qq.astype(F32) * xi).astype(BF16)
        y = (jnp.dot(p, v_ref[rows, :], preferred_element_type=F32)
             + jnp.dot(qx, st_s[n], preferred_element_type=F32))
        mu = jnp.mean(y, axis=-1, keepdims=True)
        yc = y - mu
        var = jnp.mean(yc * yc, axis=-1, keepdims=True)
        yn = yc * lax.rsqrt(var + NORM_EPS)
        o_ref[rows, :] = (sg_ref[rows, :].astype(F32) * (yn * gn)).astype(BF16)
        return carry

    lax.fori_loop(0, n_chunks, out, 0, unroll=True)


def _retention(proj, decay_logit, gn_g):
    b, s, _ = proj.shape
    n_chunks = s // RET_CHUNK

    def col(tile):
        return lambda i, h, *_: (i, 0, tile * LBLK + h)

    grid_spec = pltpu.PrefetchScalarGridSpec(
        num_scalar_prefetch=1,
        grid=(b, RET_HEADS),
        in_specs=[pl.BlockSpec((None, s, LANES), col(t)) for t in (TILE_RET_Q, TILE_RET_K, TILE_RET_V, TILE_RET_G)]
        + [pl.BlockSpec((1, LANES), lambda i, h, *_: (0, h))],
        out_specs=pl.BlockSpec((None, s, LANES), lambda i, h, *_: (i, 0, h)),
        scratch_shapes=[pltpu.VMEM((n_chunks, 2 * RET_QK_DIM, RET_V_DIM), F32),
                        pltpu.VMEM((n_chunks, 2 * RET_QK_DIM, RET_V_DIM), BF16)],
    )
    return pl.pallas_call(
        _ret_kernel,
        grid_spec=grid_spec,
        out_shape=jax.ShapeDtypeStruct((b, s, RET_HEADS * RET_V_DIM), BF16),
        compiler_params=pltpu.CompilerParams(
            dimension_semantics=("arbitrary", "arbitrary"), vmem_limit_bytes=VMEM_LIMIT_BYTES),
        name="retention",
    )(decay_logit.astype(F32).reshape(-1), proj, proj, proj, proj, gn_g.reshape(1, -1))


def _mem_kernel(mem_ref, mg_ref, wkv_ref, kg_ref, q_ref, o_ref, k_s, v_s):
    seq = o_ref.shape[0]
    hm = _rms(mem_ref[...], mg_ref[...]).astype(BF16)
    kv = jnp.dot(hm, wkv_ref[...], preferred_element_type=F32)
    kg = kg_ref[...]
    for h in range(MEM_HEADS):
        c0 = h * MEM_HEAD_DIM
        k_s[h] = _rms(kv[:, c0:c0 + MEM_HEAD_DIM], kg).astype(BF16)
        v_s[h] = kv[:, MEM_WIDTH + c0:MEM_WIDTH + c0 + MEM_HEAD_DIM].astype(BF16)

    rb = 256

    def block(i, carry):
        rows = pl.ds(pl.multiple_of(i * rb, rb), rb)
        for h in range(MEM_HEADS):
            c0 = h * MEM_HEAD_DIM
            q = q_ref[rows, c0:c0 + MEM_HEAD_DIM]
            sc = lax.dot_general(q, k_s[h], (((1,), (1,)), ((), ())), preferred_element_type=F32)
            m = jnp.max(sc, axis=1, keepdims=True)
            p = jnp.exp(sc - m)
            dsum = jnp.sum(p, axis=1, keepdims=True)
            o = jnp.dot(p.astype(BF16), v_s[h], preferred_element_type=F32)
            o_ref[rows, c0:c0 + MEM_HEAD_DIM] = (o / dsum).astype(BF16)
        return carry

    lax.fori_loop(0, seq // rb, block, 0, unroll=2)


def _memory_attention(proj, mem, mem_norm_g, w_mem_kv, mem_k_g):
    b, s, _ = proj.shape
    m, d = mem.shape[1:]
    return pl.pallas_call(
        _mem_kernel,
        grid=(b,),
        in_specs=[
            pl.BlockSpec((None, m, d), lambda i: (i, 0, 0)),
            pl.BlockSpec((1, d), lambda i: (0, 0)),
            pl.BlockSpec((d, 2 * MEM_WIDTH), lambda i: (0, 0)),
            pl.BlockSpec((1, MEM_HEAD_DIM), lambda i: (0, 0)),
            pl.BlockSpec((None, s, TN), lambda i: (i, 0, TILE_MEMQ)),
        ],
        out_specs=pl.BlockSpec((None, s, MEM_WIDTH), lambda i: (i, 0, 0)),
        out_shape=jax.ShapeDtypeStruct((b, s, MEM_WIDTH), BF16),
        scratch_shapes=[pltpu.VMEM((MEM_HEADS, m, MEM_HEAD_DIM), BF16), pltpu.VMEM((MEM_HEADS, m, MEM_HEAD_DIM), BF16)],
        compiler_params=pltpu.CompilerParams(
            dimension_semantics=("arbitrary",), vmem_limit_bytes=VMEM_LIMIT_BYTES),
        name="memory_attention",
    )(mem, mem_norm_g.reshape(1, d), w_mem_kv.astype(BF16), mem_k_g.reshape(1, -1), proj)


TM_MERGE = 1024


def _merge_kernel(x_ref, g1_ref, wg_ref, yd_ref, yr_ref, ym_ref, wd_ref, wr_ref, wm_ref, wo_ref, g2_ref,
                  x1_ref, h2_ref):
    d = x_ref.shape[1]
    x = x_ref[...]
    h = _rms(x, g1_ref[...]).astype(BF16)
    merged = None
    for j, (y_ref, w_ref) in enumerate(((yd_ref, wd_ref), (yr_ref, wr_ref), (ym_ref, wm_ref))):
        gate = jax.nn.sigmoid(jnp.dot(h, wg_ref[:, j * d:(j + 1) * d], preferred_element_type=F32))
        term = gate * jnp.dot(y_ref[...], w_ref[...], preferred_element_type=F32)
        merged = term if merged is None else merged + term
    x1 = x + jnp.dot(merged.astype(BF16), wo_ref[...], preferred_element_type=F32)
    x1_ref[...] = x1
    h2_ref[...] = _rms(x1, g2_ref[...]).astype(BF16)


def _merge_out(x, norm1_g, w_gate, y_dil, y_ret, y_mem, w_bd, w_br, w_bm, w_out, norm2_g):
    b, s, d = x.shape
    t = b * s
    tm = TM_MERGE
    row = lambda w: pl.BlockSpec((tm, w), lambda i: (i, 0))
    full = lambda r, c: pl.BlockSpec((r, c), lambda i: (0, 0), pipeline_mode=pl.Buffered(1))
    x1, h2 = pl.pallas_call(
        _merge_kernel,
        grid=(t // tm,),
        in_specs=[row(d), full(1, d), full(d, 3 * d), row(DIL_WIDTH), row(RET_HEADS * RET_V_DIM), row(MEM_WIDTH),
                  full(DIL_WIDTH, d), full(RET_HEADS * RET_V_DIM, d), full(MEM_WIDTH, d), full(d, d), full(1, d)],
        out_specs=[row(d), row(d)],
        out_shape=[jax.ShapeDtypeStruct((t, d), F32), jax.ShapeDtypeStruct((t, d), BF16)],
        compiler_params=pltpu.CompilerParams(
            dimension_semantics=("arbitrary",), vmem_limit_bytes=VMEM_LIMIT_BYTES),
        name="merge_out_projection",
    )(x.reshape(t, d), norm1_g.reshape(1, d), w_gate.astype(BF16), y_dil.reshape(t, -1), y_ret.reshape(t, -1),
      y_mem.reshape(t, -1), w_bd.astype(BF16), w_br.astype(BF16), w_bm.astype(BF16), w_out.astype(BF16),
      norm2_g.reshape(1, d))
    return x1.reshape(b, s, d), h2.reshape(b, s, d)


TM_FFN = 1024
HALO = 16
FC = 256


def _ffn_kernel(h2_ref, hp_ref, hn_ref, x1_ref, wi_ref, cw_ref, cb_ref, wo_ref, o_ref, y_s):
    i = pl.program_id(1)
    tm = h2_ref.shape[0]
    d_ff = wo_ref.shape[0]
    h = h2_ref[...]
    prev = jnp.where(i == 0, jnp.zeros_like(hp_ref[...]), hp_ref[...])
    nxt = jnp.where(i == pl.num_programs(1) - 1, jnp.zeros_like(hn_ref[...]), hn_ref[...])
    hx = jnp.concatenate([prev, h, nxt], axis=0)
    rows = tm + 2 * HALO
    for c0 in range(0, d_ff, FC):
        u = jnp.dot(hx, wi_ref[:, c0:c0 + FC], preferred_element_type=F32)
        gt = jnp.dot(h, wi_ref[:, d_ff + c0:d_ff + c0 + FC], preferred_element_type=F32)
        cw = cw_ref[:, c0:c0 + FC]
        u_prev = pltpu.roll(u, 1, axis=0)[HALO:HALO + tm]
        u_next = pltpu.roll(u, rows - 1, axis=0)[HALO:HALO + tm]
        c = cb_ref[:, c0:c0 + FC] + u_prev * cw[0:1] + u[HALO:HALO + tm] * cw[1:2] + u_next * cw[2:3]
        y = (0.5 * c * (1.0 + lax.erf(c * (2.0 ** -0.5)))) * gt
        y_s[:, c0:c0 + FC] = y.astype(BF16)
    o_ref[...] = x1_ref[...] + jnp.dot(y_s[...], wo_ref[...], preferred_element_type=F32)


def _ffn(h2, x1, w_ffn_in, conv_w, conv_b, w_ffn_out):
    b, s, d = x1.shape
    tm = TM_FFN
    nt = s // tm
    hb = tm // HALO
    const = dict(pipeline_mode=pl.Buffered(1))
    return pl.pallas_call(
        _ffn_kernel,
        grid=(b, nt),
        in_specs=[
            pl.BlockSpec((None, tm, d), lambda j, i: (j, i, 0)),
            pl.BlockSpec((None, HALO, d), lambda j, i: (j, jnp.maximum(i * hb - 1, 0), 0)),
            pl.BlockSpec((None, HALO, d), lambda j, i: (j, jnp.minimum((i + 1) * hb, s // HALO - 1), 0)),
            pl.BlockSpec((None, tm, d), lambda j, i: (j, i, 0)),
            pl.BlockSpec((d, 2 * D_FF), lambda j, i: (0, 0), **const),
            pl.BlockSpec((3, D_FF), lambda j, i: (0, 0), **const),
            pl.BlockSpec((1, D_FF), lambda j, i: (0, 0), **const),
            pl.BlockSpec((D_FF, d), lambda j, i: (0, 0), **const),
        ],
        out_specs=pl.BlockSpec((None, tm, d), lambda j, i: (j, i, 0)),
        out_shape=jax.ShapeDtypeStruct((b, s, d), F32),
        scratch_shapes=[pltpu.VMEM((tm, D_FF), BF16)],
        compiler_params=pltpu.CompilerParams(
            dimension_semantics=("arbitrary", "arbitrary"), vmem_limit_bytes=VMEM_LIMIT_BYTES),
        name="conv_glu_ffn",
    )(h2, h2, h2, x1, w_ffn_in.astype(BF16), conv_w, conv_b.reshape(1, -1), w_ffn_out.astype(BF16))


def kernel(x, mem, norm1_g, w_in, dil_q_norm_g, dil_k_norm_g, ret_decay_logit, ret_gn_g, mem_norm_g, w_mem_kv,
           mem_q_norm_g, mem_k_norm_g, w_branch_dil, w_branch_ret, w_branch_mem, w_out, norm2_g, w_ffn_in,
           ffn_conv_w, ffn_conv_b, w_ffn_out):
    depth = w_in.shape[0]
    gate0 = DIL_COLS + RET_COLS + MEM_WIDTH
    for l in range(depth):
        proj = _in_projection(x, norm1_g[l], w_in[l], dil_q_norm_g[l], dil_k_norm_g[l], mem_q_norm_g[l])
        y_dil = _dilated_attention(proj)
        y_ret = _retention(proj, ret_decay_logit[l], ret_gn_g[l])
        y_mem = _memory_attention(proj, mem, mem_norm_g[l], w_mem_kv[l], mem_k_norm_g[l])
        x1, h2 = _merge_out(x, norm1_g[l], w_in[l][:, gate0:], y_dil, y_ret, y_mem, w_branch_dil[l],
                            w_branch_ret[l], w_branch_mem[l], w_out[l], norm2_g[l])
        x = _ffn(h2, x1, w_ffn_in[l], ffn_conv_w[l], ffn_conv_b[l], w_ffn_out[l])
    return x
```

```python
import functools

import numpy as np
import jax
import jax.numpy as jnp
from jax import lax
from jax.experimental import pallas as pl
from jax.experimental.pallas import tpu as pltpu

F32 = jnp.float32
BF16 = jnp.bfloat16

D_MODEL = 1024
SEQ = 2048
N_MEM = 256
DIL_PATTERNS = ((128, 1), (512, 4), (2048, 16))
DIL_HEADS = 8
DIL_HEAD_DIM = 64
DIL_WIDTH = DIL_HEADS * DIL_HEAD_DIM
DIL_HALF = 64
RET_HEADS = 4
RET_QK_DIM = 64
RET_V_DIM = 128
RET_CHUNK = 128
MEM_HEADS = 4
MEM_HEAD_DIM = 128
MEM_WIDTH = MEM_HEADS * MEM_HEAD_DIM
D_FF = 2816
NORM_EPS = 1e-6
MASK_VALUE = -1e30
LOG2E = 1.4426950408889634
DIL_COLS = 3 * 3 * DIL_WIDTH
RET_COLS = 2 * RET_HEADS * RET_QK_DIM + 2 * RET_HEADS * RET_V_DIM

LANES = 128
MXU_DIM = 256
VMEM_LIMIT_BYTES = 56 * 1024 * 1024

TN = 512
SUB = 2
EPI_SCALE, EPI_NORM64, EPI_NORM128, EPI_SILU = 0, 1, 2, 3
N_TILES = 14
TILE_RET_Q, TILE_RET_K, TILE_RET_V, TILE_RET_G, TILE_MEMQ = 9, 10, 11, 12, 13
TILE_EPILOGUE = (EPI_NORM64, EPI_NORM64, EPI_SCALE) * 3 + (EPI_SCALE,) * 3 + (EPI_SILU, EPI_NORM128)
TILE_SLOT = (0, 0, 0, 1, 1, 1, 2, 2, 2) + (0,) * 5
LBLK = TN // LANES


def _rms(xf, g):
    ms = jnp.mean(xf * xf, axis=-1, keepdims=True)
    return xf * lax.rsqrt(ms + NORM_EPS) * g


def _inproj_kernel(kind_ref, slot_ref, *refs, kinds):
    n_slabs = len(refs) - 7
    xs = refs[:n_slabs]
    g1_ref, w_ref, gv_ref, bm_ref, o_ref, hs_ref, hp_ref = refs[n_slabs:]
    n = pl.program_id(1)
    seq = xs[0].shape[0]

    @pl.when(n == 0)
    def _prologue():
        g = g1_ref[...]
        rc = 256

        def normed(rows):
            return _rms(jnp.concatenate([x[rows, :] for x in xs], axis=1), g)

        def nat(i, c):
            rows = pl.ds(pl.multiple_of(i * rc, rc), rc)
            hs_ref[0, rows, :] = normed(rows).astype(BF16)
            return c

        lax.fori_loop(0, seq // rc, nat, 0, unroll=2)

        for slot in range(1, len(DIL_PATTERNS)):
            r_prev, r = DIL_PATTERNS[slot - 1][1], DIL_PATTERNS[slot][1]
            ratio = r // r_prev
            n_sub_prev, n_sub = seq // r_prev, seq // r
            rows = min(rc, n_sub)
            per = n_sub // rows
            last = slot + 1 == len(DIL_PATTERNS)

            def stage(i, c, slot=slot, ratio=ratio, n_sub_prev=n_sub_prev, n_sub=n_sub, rows=rows, per=per,
                      last=last):
                cls = i // per
                j = i % per
                src = pl.ds((cls // ratio) * n_sub_prev + cls % ratio + ratio * rows * j, rows, stride=ratio)
                dst = pl.ds(pl.multiple_of(cls * n_sub + j * rows, rows), rows)
                if slot == 1:
                    hn = normed(src)
                else:
                    hn = jnp.concatenate([hp_ref[k, src, :] for k in range(n_slabs)], axis=1)
                hs_ref[slot, dst, :] = hn.astype(BF16)
                if not last:
                    for k in range(n_slabs):
                        hp_ref[k, dst, :] = hn[:, k * LANES:(k + 1) * LANES]
                return c

            lax.fori_loop(0, r * per, stage, 0, unroll=2)

    def head_norm(acc, gv, which, head_dim):
        sq = (acc * acc).astype(BF16)
        ms = jnp.concatenate(
            [jnp.dot(sq[:, c0:c0 + MXU_DIM], bm_ref[which], preferred_element_type=F32)
             for c0 in range(0, TN, MXU_DIM)], axis=1)
        return acc * lax.rsqrt(ms * (1.0 / head_dim) + NORM_EPS) * gv

    epilogues = {
        EPI_SCALE: lambda acc, gv: acc * gv,
        EPI_NORM64: lambda acc, gv: head_norm(acc, gv, 0, DIL_HEAD_DIM),
        EPI_NORM128: lambda acc, gv: head_norm(acc, gv, 1, MEM_HEAD_DIM),
        EPI_SILU: lambda acc, gv: acc * jax.nn.sigmoid(acc),
    }
    rm = 512
    for sub in range(SUB):
        kind = kind_ref[n * SUB + sub]
        slot = slot_ref[n * SUB + sub]
        cols = slice(sub * TN, (sub + 1) * TN)
        for epi in kinds[sub]:

            @pl.when(kind == epi)
            def _(epi=epi, slot=slot, cols=cols, sub=sub):
                gv = gv_ref[sub]
                for r0 in range(0, seq, rm):
                    acc = jnp.dot(hs_ref[slot, r0:r0 + rm, :], w_ref[:, cols], preferred_element_type=F32)
                    o_ref[r0:r0 + rm, cols] = epilogues[epi](acc, gv).astype(BF16)


def _block_diag_ones(block):
    i = np.arange(MXU_DIM)
    return (i[:, None] // block == i[None, :] // block).astype(np.float32)


def _in_projection(x, norm1_g, w_in, dil_q_g, dil_k_g, mem_q_g):
    b, s, d = x.shape
    assert len(DIL_PATTERNS) == 3
    o = DIL_COLS
    wq = w_in[:, o:o + 256].reshape(d, RET_HEADS, 1, RET_QK_DIM)
    wk = w_in[:, o + 256:o + 512].reshape(d, RET_HEADS, 1, RET_QK_DIM)
    dup = lambda w: jnp.broadcast_to(w, (d, RET_HEADS, 2, RET_QK_DIM)).reshape(d, TN)
    w_all = jnp.concatenate(
        [w_in[:, :DIL_COLS], dup(wq), dup(wk), w_in[:, o + 512:o + 1536],
         w_in[:, o + RET_COLS:o + RET_COLS + MEM_WIDTH]], axis=1).astype(BF16)

    ones = jnp.ones((TN,), F32)
    gv = []
    for g in range(3):
        gv += [jnp.tile(dil_q_g[g], DIL_HEADS) * (DIL_HEAD_DIM ** -0.5 * LOG2E), jnp.tile(dil_k_g[g], DIL_HEADS),
               ones]
    gv += [ones, ones * (RET_QK_DIM ** -0.5), ones, ones, jnp.tile(mem_q_g, MEM_HEADS) * (MEM_HEAD_DIM ** -0.5)]
    gv = jnp.stack(gv).reshape(N_TILES, 1, TN).astype(F32)
    bm = jnp.asarray(np.stack([_block_diag_ones(DIL_HEAD_DIM), _block_diag_ones(MEM_HEAD_DIM)]), BF16)
    kinds = tuple(tuple(sorted(set(TILE_EPILOGUE[sub::SUB]))) for sub in range(SUB))

    grid_spec = pltpu.PrefetchScalarGridSpec(
        num_scalar_prefetch=2,
        grid=(b, N_TILES // SUB),
        in_specs=[
            pl.BlockSpec((None, s, LANES), functools.partial(lambda i, n, *_, k: (i, 0, k), k=k))
            for k in range(d // LANES)
        ] + [
            pl.BlockSpec((1, d), lambda i, n, *_: (0, 0)),
            pl.BlockSpec((d, SUB * TN), lambda i, n, *_: (0, n)),
            pl.BlockSpec((SUB, 1, TN), lambda i, n, *_: (n, 0, 0)),
            pl.BlockSpec((2, MXU_DIM, MXU_DIM), lambda i, n, *_: (0, 0, 0)),
        ],
        out_specs=pl.BlockSpec((None, s, SUB * TN), lambda i, n, *_: (i, 0, n)),
        scratch_shapes=[pltpu.VMEM((3, s, d), BF16), pltpu.VMEM((d // LANES, s, LANES), F32)],
    )
    return pl.pallas_call(
        functools.partial(_inproj_kernel, kinds=kinds),
        grid_spec=grid_spec,
        out_shape=jax.ShapeDtypeStruct((b, s, N_TILES * TN), BF16),
        compiler_params=pltpu.CompilerParams(
            dimension_semantics=("arbitrary", "arbitrary"), vmem_limit_bytes=VMEM_LIMIT_BYTES),
        name="in_projection",
    )(jnp.asarray(TILE_EPILOGUE, jnp.int32), jnp.asarray(TILE_SLOT, jnp.int32), *([x] * (d // LANES)),
      norm1_g.reshape(1, d), w_all, gv, bm)


QB = 128
KW = 256


def _dil_kernel(slope_ref, *refs):
    qkv = refs[:9]
    o_ref = refs[9]
    state = (refs[10:13], refs[13:16])
    bias_s = refs[16]
    hp = pl.program_id(1)
    seq = o_ref.shape[0]
    s0 = slope_ref[2 * hp]
    s1 = slope_ref[2 * hp + 1]

    lane = lax.broadcasted_iota(jnp.int32, (QB, LANES), 1)
    first = lane < DIL_HEAD_DIM
    ones = jnp.ones((KW, LANES), BF16)

    for g, (_, r) in enumerate(DIL_PATTERNS):
        q_ref, k_ref, v_ref = qkv[3 * g:3 * g + 3]
        n_sub = seq // r
        per = n_sub // QB
        kw = min(KW, n_sub)
        r_prev = DIL_PATTERNS[g - 1][1] if g else 1
        ratio = r // r_prev
        src = state[(g - 1) % 2]
        dst = state[g % 2]

        row = lax.broadcasted_iota(jnp.int32, (2 * QB, kw), 0)
        col = lax.broadcasted_iota(jnp.int32, (2 * QB, kw), 1)
        base = (col - (row & (QB - 1))).astype(F32)
        slope = jnp.where(row >= QB, s1, s0) * (float(r) * LOG2E)
        offsets = (0, -DIL_HALF, -2 * DIL_HALF) if kw == KW else (0,)
        for vi, off in enumerate(offsets):
            a = jnp.abs(base + float(off))
            bias_s[vi, :, :kw] = jnp.where(a <= float(DIL_HALF), -slope * a, MASK_VALUE)

        def block(t, carry, g=g, r_prev=r_prev, ratio=ratio, n_sub=n_sub, per=per, kw=kw, src=src, dst=dst,
                  q_ref=q_ref, k_ref=k_ref, v_ref=v_ref):
            cls = t // per
            qb = t % per
            rows = pl.ds(pl.multiple_of(t * QB, QB), QB)
            if kw == KW:
                ks = jnp.clip(qb * QB - DIL_HALF, 0, n_sub - KW)
                kabs = pl.multiple_of(cls * n_sub + ks, DIL_HALF)
                vi = jnp.where(qb == 0, 0, jnp.where(qb == per - 1, 2, 1))
            else:
                kabs = pl.multiple_of(t * QB, QB)
                vi = 0
            q = q_ref[rows, :]
            kwin = k_ref[pl.ds(kabs, kw), :]
            vwin = v_ref[pl.ds(kabs, kw), :]
            zero = jnp.zeros_like(q)
            q2 = jnp.concatenate([jnp.where(first, q, zero), jnp.where(first, zero, q)], axis=0)
            sc = lax.dot_general(q2, kwin, (((1,), (1,)), ((), ())), preferred_element_type=F32)
            sc = sc + bias_s[vi, :, :kw]
            m = jnp.max(sc, axis=1, keepdims=True)
            p = jnp.exp2(sc - m).astype(BF16)
            od = jnp.dot(p, jnp.concatenate([vwin, ones[:kw]], axis=1), preferred_element_type=F32)
            o = od[:, :LANES]
            dd = od[:, LANES:]
            accb = jnp.where(first, o[:QB], o[QB:])
            mb = jnp.where(first, m[:QB], m[QB:])
            db = jnp.where(first, dd[:QB], dd[QB:])
            if g == 0:
                dst[0][rows, :] = accb
                dst[1][rows, :] = mb
                dst[2][rows, :] = db
            else:
                n_sub_prev = seq // r_prev
                start = (cls // ratio) * n_sub_prev + cls % ratio + ratio * QB * qb
                prev = pl.ds(start, QB, stride=ratio)
                m_old = src[1][prev, :]
                m_new = jnp.maximum(m_old, mb)
                a_old = jnp.exp2(m_old - m_new)
                a_blk = jnp.exp2(mb - m_new)
                dst[0][rows, :] = a_old * src[0][prev, :] + a_blk * accb
                dst[2][rows, :] = a_old * src[2][prev, :] + a_blk * db
                if g + 1 < len(DIL_PATTERNS):
                    dst[1][rows, :] = m_new
            return carry

        lax.fori_loop(0, seq // QB, block, 0, unroll=True)

    last = len(DIL_PATTERNS) - 1
    r = DIL_PATTERNS[last][1]
    per = seq // r // QB
    acc_ref, _, den_ref = state[last % 2]
    nat_ref = state[(last + 1) % 2][0]

    def unpermute(t, c):
        rows = pl.ds(pl.multiple_of(t * QB, QB), QB)
        cls = t // per
        residue = 0
        for gg in range(last, 0, -1):
            ratio = DIL_PATTERNS[gg][1] // DIL_PATTERNS[gg - 1][1]
            residue = residue + DIL_PATTERNS[gg - 1][1] * (cls % ratio)
            cls = cls // ratio
        nat_ref[pl.ds(residue + r * QB * (t % per), QB, stride=r), :] = acc_ref[rows, :] / den_ref[rows, :]
        return c

    lax.fori_loop(0, seq // QB, unpermute, 0, unroll=2)
    rc = 256

    def fin(i, c):
        rows = pl.ds(pl.multiple_of(i * rc, rc), rc)
        o_ref[rows, :] = nat_ref[rows, :].astype(BF16)
        return c

    lax.fori_loop(0, seq // rc, fin, 0, unroll=2)


def _dilated_attention(proj):
    b, s, _ = proj.shape
    assert DIL_PATTERNS[0][1] == 1 and all(w // (2 * r) == DIL_HALF for w, r in DIL_PATTERNS)
    slopes = jnp.exp2(-jnp.arange(1, DIL_HEADS + 1, dtype=F32) * (8.0 / DIL_HEADS))
    in_specs = [
        pl.BlockSpec((None, s, LANES), functools.partial(lambda i, hp, *_, t: (i, 0, t * LBLK + hp), t=t))
        for t in range(9)
    ]
    grid_spec = pltpu.PrefetchScalarGridSpec(
        num_scalar_prefetch=1,
        grid=(b, DIL_HEADS // 2),
        in_specs=in_specs,
        out_specs=pl.BlockSpec((None, s, LANES), lambda i, hp, *_: (i, 0, hp)),
        scratch_shapes=[pltpu.VMEM((s, LANES), F32)] * 6 + [pltpu.VMEM((3, 2 * QB, KW), F32)],
    )
    return pl.pallas_call(
        _dil_kernel,
        grid_spec=grid_spec,
        out_shape=jax.ShapeDtypeStruct((b, s, DIL_WIDTH), BF16),
        compiler_params=pltpu.CompilerParams(
            dimension_semantics=("arbitrary", "arbitrary"), vmem_limit_bytes=VMEM_LIMIT_BYTES),
        name="dilated_attention",
    )(slopes, *([proj] * 9))


def _ret_kernel(dl_ref, qq_ref, kk_ref, v_ref, sg_ref, gn_ref, o_ref, u_s, st_s):
    for h in range(RET_HEADS):
        lanes = pl.ds(h * LANES, LANES)
        _ret_head(h, dl_ref, qq_ref.at[:, lanes], kk_ref.at[:, lanes], v_ref.at[:, lanes], sg_ref.at[:, lanes],
                  gn_ref.at[:, lanes], o_ref.at[:, lanes], u_s.at[h], st_s.at[h])


def _ret_head(h, dl_ref, qq_ref, kk_ref, v_ref, sg_ref, gn_ref, o_ref, u_s, st_s):
    seq = o_ref.shape[0]
    c = RET_CHUNK
    n_chunks = seq // c
    dk = RET_QK_DIM

    def log_sigmoid(z):
        return jnp.minimum(z, 0.0) - jnp.log1p(jnp.exp(-jnp.abs(z)))

    lgf = log_sigmoid(jnp.full((1, LANES), dl_ref[h], F32))
    lgb = log_sigmoid(jnp.full((1, LANES), dl_ref[RET_HEADS + h], F32))
    ii = lax.broadcasted_iota(jnp.int32, (c, LANES), 0).astype(F32)
    jj = lax.broadcasted_iota(jnp.int32, (c, LANES), 1).astype(F32)
    first = lax.broadcasted_iota(jnp.int32, (c, LANES), 1) < dk
    diff = ii - jj
    decay = jnp.exp(jnp.where(diff >= 0.0, lgf * diff, -lgb * diff))
    xi = jnp.exp(jnp.where(first, lgf * (ii + 1.0), lgb * (float(c) - ii)))
    zeta = jnp.exp(jnp.where(first, lgf * (float(c - 1) - ii), lgb * ii))
    chunk_f = jnp.exp(lgf * float(c))
    chunk_b = jnp.exp(lgb * float(c))

    def rows_of(n):
        return pl.ds(pl.multiple_of(n * c, c), c)

    def incr(n, carry):
        kz = (kk_ref[rows_of(n), :].astype(F32) * zeta).T.astype(BF16)
        u_s[n] = jnp.dot(kz, v_ref[rows_of(n), :], preferred_element_type=F32)
        return carry

    lax.fori_loop(0, n_chunks, incr, 0, unroll=True)

    def scan_f(n, state):
        st_s[n, :dk, :] = state.astype(BF16)
        return state * chunk_f + u_s[n, :dk, :]

    lax.fori_loop(0, n_chunks, scan_f, jnp.zeros((dk, LANES), F32))

    def scan_b(i, state):
        n = n_chunks - 1 - i
        st_s[n, dk:, :] = state.astype(BF16)
        return state * chunk_b + u_s[n, dk:, :]

    lax.fori_loop(0, n_chunks, scan_b, jnp.zeros((dk, LANES), F32))

    gn = gn_ref[...]

    def out(n, carry):
        rows = rows_of(n)
        qq = qq_ref[rows, :]
        kk = kk_ref[rows, :]
        q0 = jnp.where(first, qq, jnp.zeros_like(qq))
        a = lax.dot_general(q0, kk, (((1,), (1,)), ((), ())), preferred_element_type=F32)
        p = (a * decay).astype(BF16)
        qx = (qq.astype(F32) * xi).astype(BF16)
        y = (jnp.dot(p, v_ref[rows, :], preferred_element_type=F32)
             + jnp.dot(qx, st_s[n], preferred_element_type=F32))
        mu = jnp.mean(y, axis=-1, keepdims=True)
        yc = y - mu
        var = jnp.mean(yc * yc, axis=-1, keepdims=True)
        yn = yc * lax.rsqrt(var + NORM_EPS)
        o_ref[rows, :] = (sg_ref[rows, :].astype(F32) * (yn * gn)).astype(BF16)
        return carry

    lax.fori_loop(0, n_chunks, out, 0, unroll=True)


def _retention(proj, decay_logit, gn_g):
    b, s, _ = proj.shape
    n_chunks = s // RET_CHUNK

    def col(tile):
        return lambda i, *_: (i, 0, tile)

    width = RET_HEADS * RET_V_DIM
    assert width == TN
    grid_spec = pltpu.PrefetchScalarGridSpec(
        num_scalar_prefetch=1,
        grid=(b,),
        in_specs=[pl.BlockSpec((None, s, TN), col(t)) for t in (TILE_RET_Q, TILE_RET_K, TILE_RET_V, TILE_RET_G)]
        + [pl.BlockSpec((1, width), lambda i, *_: (0, 0))],
        out_specs=pl.BlockSpec((None, s, width), lambda i, *_: (i, 0, 0)),
        scratch_shapes=[pltpu.VMEM((RET_HEADS, n_chunks, 2 * RET_QK_DIM, RET_V_DIM), F32),
                        pltpu.VMEM((RET_HEADS, n_chunks, 2 * RET_QK_DIM, RET_V_DIM), BF16)],
    )
    return pl.pallas_call(
        _ret_kernel,
        grid_spec=grid_spec,
        out_shape=jax.ShapeDtypeStruct((b, s, width), BF16),
        compiler_params=pltpu.CompilerParams(
            dimension_semantics=("arbitrary",), vmem_limit_bytes=VMEM_LIMIT_BYTES),
        name="retention",
    )(decay_logit.astype(F32).reshape(-1), proj, proj, proj, proj, gn_g.reshape(1, -1))


def _mem_kernel(mem_ref, mg_ref, wkv_ref, kg_ref, q_ref, o_ref, k_s, v_s):
    seq = o_ref.shape[0]
    hm = _rms(mem_ref[...], mg_ref[...]).astype(BF16)
    kv = jnp.dot(hm, wkv_ref[...], preferred_element_type=F32)
    kg = kg_ref[...]
    for h in range(MEM_HEADS):
        c0 = h * MEM_HEAD_DIM
        k_s[h] = _rms(kv[:, c0:c0 + MEM_HEAD_DIM], kg).astype(BF16)
        v_s[h] = kv[:, MEM_WIDTH + c0:MEM_WIDTH + c0 + MEM_HEAD_DIM].astype(BF16)

    rb = 256

    def block(i, carry):
        rows = pl.ds(pl.multiple_of(i * rb, rb), rb)
        for h in range(MEM_HEADS):
            c0 = h * MEM_HEAD_DIM
            q = q_ref[rows, c0:c0 + MEM_HEAD_DIM]
            sc = lax.dot_general(q, k_s[h], (((1,), (1,)), ((), ())), preferred_element_type=F32)
            m = jnp.max(sc, axis=1, keepdims=True)
            p = jnp.exp(sc - m)
            dsum = jnp.sum(p, axis=1, keepdims=True)
            o = jnp.dot(p.astype(BF16), v_s[h], preferred_element_type=F32)
            o_ref[rows, c0:c0 + MEM_HEAD_DIM] = (o / dsum).astype(BF16)
        return carry

    lax.fori_loop(0, seq // rb, block, 0, unroll=2)


def _memory_attention(proj, mem, mem_norm_g, w_mem_kv, mem_k_g):
    b, s, _ = proj.shape
    m, d = mem.shape[1:]
    return pl.pallas_call(
        _mem_kernel,
        grid=(b,),
        in_specs=[
            pl.BlockSpec((None, m, d), lambda i: (i, 0, 0)),
            pl.BlockSpec((1, d), lambda i: (0, 0)),
            pl.BlockSpec((d, 2 * MEM_WIDTH), lambda i: (0, 0)),
            pl.BlockSpec((1, MEM_HEAD_DIM), lambda i: (0, 0)),
            pl.BlockSpec((None, s, TN), lambda i: (i, 0, TILE_MEMQ)),
        ],
        out_specs=pl.BlockSpec((None, s, MEM_WIDTH), lambda i: (i, 0, 0)),
        out_shape=jax.ShapeDtypeStruct((b, s, MEM_WIDTH), BF16),
        scratch_shapes=[pltpu.VMEM((MEM_HEADS, m, MEM_HEAD_DIM), BF16), pltpu.VMEM((MEM_HEADS, m, MEM_HEAD_DIM), BF16)],
        compiler_params=pltpu.CompilerParams(
            dimension_semantics=("arbitrary",), vmem_limit_bytes=VMEM_LIMIT_BYTES),
        name="memory_attention",
    )(mem, mem_norm_g.reshape(1, d), w_mem_kv.astype(BF16), mem_k_g.reshape(1, -1), proj)


TM_MERGE = 1024


def _merge_kernel(x_ref, g1_ref, wg_ref, yd_ref, yr_ref, ym_ref, wd_ref, wr_ref, wm_ref, wo_ref, g2_ref,
                  x1_ref, h2_ref):
    d = x_ref.shape[1]
    x = x_ref[...]
    h = _rms(x, g1_ref[...]).astype(BF16)
    merged = None
    for j, (y_ref, w_ref) in enumerate(((yd_ref, wd_ref), (yr_ref, wr_ref), (ym_ref, wm_ref))):
        gate = jax.nn.sigmoid(jnp.dot(h, wg_ref[:, j * d:(j + 1) * d], preferred_element_type=F32))
        term = gate * jnp.dot(y_ref[...], w_ref[...], preferred_element_type=F32)
        merged = term if merged is None else merged + term
    x1 = x + jnp.dot(merged.astype(BF16), wo_ref[...], preferred_element_type=F32)
    x1_ref[...] = x1
    h2_ref[...] = _rms(x1, g2_ref[...]).astype(BF16)


def _merge_out(x, norm1_g, w_gate, y_dil, y_ret, y_mem, w_bd, w_br, w_bm, w_out, norm2_g):
    b, s, d = x.shape
    t = b * s
    tm = TM_MERGE
    row = lambda w: pl.BlockSpec((tm, w), lambda i: (i, 0))
    full = lambda r, c: pl.BlockSpec((r, c), lambda i: (0, 0), pipeline_mode=pl.Buffered(1))
    x1, h2 = pl.pallas_call(
        _merge_kernel,
        grid=(t // tm,),
        in_specs=[row(d), full(1, d), full(d, 3 * d), row(DIL_WIDTH), row(RET_HEADS * RET_V_DIM), row(MEM_WIDTH),
                  full(DIL_WIDTH, d), full(RET_HEADS * RET_V_DIM, d), full(MEM_WIDTH, d), full(d, d), full(1, d)],
        out_specs=[row(d), row(d)],
        out_shape=[jax.ShapeDtypeStruct((t, d), F32), jax.ShapeDtypeStruct((t, d), BF16)],
        compiler_params=pltpu.CompilerParams(
            dimension_semantics=("arbitrary",), vmem_limit_bytes=VMEM_LIMIT_BYTES),
        name="merge_out_projection",
    )(x.reshape(t, d), norm1_g.reshape(1, d), w_gate.astype(BF16), y_dil.reshape(t, -1), y_ret.reshape(t, -1),
      y_mem.reshape(t, -1), w_bd.astype(BF16), w_br.astype(BF16), w_bm.astype(BF16), w_out.astype(BF16),
      norm2_g.reshape(1, d))
    return x1.reshape(b, s, d), h2.reshape(b, s, d)


TM_FFN = 1024
HALO = 16
FC = 256


def _ffn_kernel(h2_ref, hp_ref, hn_ref, x1_ref, wi_ref, cw_ref, cb_ref, wo_ref, o_ref, y_s):
    i = pl.program_id(1)
    tm = h2_ref.shape[0]
    d_ff = wo_ref.shape[0]
    h = h2_ref[...]
    prev = jnp.where(i == 0, jnp.zeros_like(hp_ref[...]), hp_ref[...])
    nxt = jnp.where(i == pl.num_programs(1) - 1, jnp.zeros_like(hn_ref[...]), hn_ref[...])
    hx = jnp.concatenate([prev, h, nxt], axis=0)
    rows = tm + 2 * HALO
    for c0 in range(0, d_ff, FC):
        u = jnp.dot(hx, wi_ref[:, c0:c0 + FC], preferred_element_type=F32)
        gt = jnp.dot(h, wi_ref[:, d_ff + c0:d_ff + c0 + FC], preferred_element_type=F32)
        cw = cw_ref[:, c0:c0 + FC]
        u_prev = pltpu.roll(u, 1, axis=0)[HALO:HALO + tm]
        u_next = pltpu.roll(u, rows - 1, axis=0)[HALO:HALO + tm]
        c = cb_ref[:, c0:c0 + FC] + u_prev * cw[0:1] + u[HALO:HALO + tm] * cw[1:2] + u_next * cw[2:3]
        y = (0.5 * c * (1.0 + lax.erf(c * (2.0 ** -0.5)))) * gt
        y_s[:, c0:c0 + FC] = y.astype(BF16)
    o_ref[...] = x1_ref[...] + jnp.dot(y_s[...], wo_ref[...], preferred_element_type=F32)


def _ffn(h2, x1, w_ffn_in, conv_w, conv_b, w_ffn_out):
    b, s, d = x1.shape
    tm = TM_FFN
    nt = s // tm
    hb = tm // HALO
    const = dict(pipeline_mode=pl.Buffered(1))
    return pl.pallas_call(
        _ffn_kernel,
        grid=(b, nt),
        in_specs=[
            pl.BlockSpec((None, tm, d), lambda j, i: (j, i, 0)),
            pl.BlockSpec((None, HALO, d), lambda j, i: (j, jnp.maximum(i * hb - 1, 0), 0)),
            pl.BlockSpec((None, HALO, d), lambda j, i: (j, jnp.minimum((i + 1) * hb, s // HALO - 1), 0)),
            pl.BlockSpec((None, tm, d), lambda j, i: (j, i, 0)),
            pl.BlockSpec((d, 2 * D_FF), lambda j, i: (0, 0), **const),
            pl.BlockSpec((3, D_FF), lambda j, i: (0, 0), **const),
            pl.BlockSpec((1, D_FF), lambda j, i: (0, 0), **const),
            pl.BlockSpec((D_FF, d), lambda j, i: (0, 0), **const),
        ],
        out_specs=pl.BlockSpec((None, tm, d), lambda j, i: (j, i, 0)),
        out_shape=jax.ShapeDtypeStruct((b, s, d), F32),
        scratch_shapes=[pltpu.VMEM((tm, D_FF), BF16)],
        compiler_params=pltpu.CompilerParams(
            dimension_semantics=("arbitrary", "arbitrary"), vmem_limit_bytes=VMEM_LIMIT_BYTES),
        name="conv_glu_ffn",
    )(h2, h2, h2, x1, w_ffn_in.astype(BF16), conv_w, conv_b.reshape(1, -1), w_ffn_out.astype(BF16))


def kernel(x, mem, norm1_g, w_in, dil_q_norm_g, dil_k_norm_g, ret_decay_logit, ret_gn_g, mem_norm_g, w_mem_kv,
           mem_q_norm_g, mem_k_norm_g, w_branch_dil, w_branch_ret, w_branch_mem, w_out, norm2_g, w_ffn_in,
           ffn_conv_w, ffn_conv_b, w_ffn_out):
    depth = w_in.shape[0]
    gate0 = DIL_COLS + RET_COLS + MEM_WIDTH
    for l in range(depth):
        proj = _in_projection(x, norm1_g[l], w_in[l], dil_q_norm_g[l], dil_k_norm_g[l], mem_q_norm_g[l])
        y_dil = _dilated_attention(proj)
        y_ret = _retention(proj, ret_decay_logit[l], ret_gn_g[l])
        y_mem = _memory_attention(proj, mem, mem_norm_g[l], w_mem_kv[l], mem_k_norm_g[l])
        x1, h2 = _merge_out(x, norm1_g[l], w_in[l][:, gate0:], y_dil, y_ret, y_mem, w_branch_dil[l],
                            w_branch_ret[l], w_branch_mem[l], w_out[l], norm2_g[l])
        x = _ffn(h2, x1, w_ffn_in[l], ffn_conv_w[l], ffn_conv_b[l], w_ffn_out[l])
    return x
```

```python
import functools

import numpy as np
import jax
import jax.numpy as jnp
from jax import lax
from jax.experimental import pallas as pl
from jax.experimental.pallas import tpu as pltpu

F32 = jnp.float32
BF16 = jnp.bfloat16

D_MODEL = 1024
SEQ = 2048
N_MEM = 256
DIL_PATTERNS = ((128, 1), (512, 4), (2048, 16))
DIL_HEADS = 8
DIL_HEAD_DIM = 64
DIL_WIDTH = DIL_HEADS * DIL_HEAD_DIM
DIL_HALF = 64
RET_HEADS = 4
RET_QK_DIM = 64
RET_V_DIM = 128
RET_CHUNK = 128
MEM_HEADS = 4
MEM_HEAD_DIM = 128
MEM_WIDTH = MEM_HEADS * MEM_HEAD_DIM
D_FF = 2816
NORM_EPS = 1e-6
MASK_VALUE = -1e30
LOG2E = 1.4426950408889634
DIL_COLS = 3 * 3 * DIL_WIDTH
RET_COLS = 2 * RET_HEADS * RET_QK_DIM + 2 * RET_HEADS * RET_V_DIM

LANES = 128
MXU_DIM = 256
VMEM_LIMIT_BYTES = 56 * 1024 * 1024

TN = 512
SUB = 2
EPI_SCALE, EPI_NORM64, EPI_NORM128, EPI_SILU = 0, 1, 2, 3
N_TILES = 14
TILE_RET_Q, TILE_RET_K, TILE_RET_V, TILE_RET_G, TILE_MEMQ = 9, 10, 11, 12, 13
TILE_EPILOGUE = (EPI_NORM64, EPI_NORM64, EPI_SCALE) * 3 + (EPI_SCALE,) * 3 + (EPI_SILU, EPI_NORM128)
TILE_SLOT = (0, 0, 0, 1, 1, 1, 2, 2, 2) + (0,) * 5


def _rms(xf, g):
    ms = jnp.mean(xf * xf, axis=-1, keepdims=True)
    return xf * lax.rsqrt(ms + NORM_EPS) * g


def _inproj_kernel(kind_ref, slot_ref, *refs, kinds):
    n_slabs = len(refs) - 7
    xs = refs[:n_slabs]
    g1_ref, w_ref, gv_ref, bm_ref, o_ref, hs_ref, hp_ref = refs[n_slabs:]
    n = pl.program_id(1)
    seq = xs[0].shape[0]

    @pl.when(n == 0)
    def _prologue():
        g = g1_ref[...]
        rc = 256

        def normed(rows):
            return _rms(jnp.concatenate([x[rows, :] for x in xs], axis=1), g)

        def nat(i, c):
            rows = pl.ds(pl.multiple_of(i * rc, rc), rc)
            hs_ref[0, rows, :] = normed(rows).astype(BF16)
            return c

        lax.fori_loop(0, seq // rc, nat, 0, unroll=2)

        for slot in range(1, len(DIL_PATTERNS)):
            r_prev, r = DIL_PATTERNS[slot - 1][1], DIL_PATTERNS[slot][1]
            ratio = r // r_prev
            n_sub_prev, n_sub = seq // r_prev, seq // r
            rows = min(rc, n_sub)
            per = n_sub // rows
            last = slot + 1 == len(DIL_PATTERNS)

            def stage(i, c, slot=slot, ratio=ratio, n_sub_prev=n_sub_prev, n_sub=n_sub, rows=rows, per=per,
                      last=last):
                cls = i // per
                j = i % per
                src = pl.ds((cls // ratio) * n_sub_prev + cls % ratio + ratio * rows * j, rows, stride=ratio)
                dst = pl.ds(pl.multiple_of(cls * n_sub + j * rows, rows), rows)
                if slot == 1:
                    hn = normed(src)
                else:
                    hn = jnp.concatenate([hp_ref[k, src, :] for k in range(n_slabs)], axis=1)
                hs_ref[slot, dst, :] = hn.astype(BF16)
                if not last:
                    for k in range(n_slabs):
                        hp_ref[k, dst, :] = hn[:, k * LANES:(k + 1) * LANES]
                return c

            lax.fori_loop(0, r * per, stage, 0, unroll=2)

    def head_norm(acc, gv, which, head_dim):
        sq = (acc * acc).astype(BF16)
        ms = jnp.concatenate(
            [jnp.dot(sq[:, c0:c0 + MXU_DIM], bm_ref[which], preferred_element_type=F32)
             for c0 in range(0, TN, MXU_DIM)], axis=1)
        return acc * lax.rsqrt(ms * (1.0 / head_dim) + NORM_EPS) * gv

    epilogues = {
        EPI_SCALE: lambda acc, gv: acc * gv,
        EPI_NORM64: lambda acc, gv: head_norm(acc, gv, 0, DIL_HEAD_DIM),
        EPI_NORM128: lambda acc, gv: head_norm(acc, gv, 1, MEM_HEAD_DIM),
        EPI_SILU: lambda acc, gv: acc * jax.nn.sigmoid(acc),
    }
    rm = 512
    for sub in range(SUB):
        kind = kind_ref[n * SUB + sub]
        slot = slot_ref[n * SUB + sub]
        cols = slice(sub * TN, (sub + 1) * TN)
        for epi in kinds[sub]:

            @pl.when(kind == epi)
            def _(epi=epi, slot=slot, cols=cols, sub=sub):
                gv = gv_ref[sub]
                for r0 in range(0, seq, rm):
                    acc = jnp.dot(hs_ref[slot, r0:r0 + rm, :], w_ref[:, cols], preferred_element_type=F32)
                    o_ref[r0:r0 + rm, cols] = epilogues[epi](acc, gv).astype(BF16)


def _block_diag_ones(block):
    i = np.arange(MXU_DIM)
    return (i[:, None] // block == i[None, :] // block).astype(np.float32)


def _in_projection(x, norm1_g, w_in, dil_q_g, dil_k_g, mem_q_g):
    b, s, d = x.shape
    assert len(DIL_PATTERNS) == 3
    o = DIL_COLS
    wq = w_in[:, o:o + 256].reshape(d, RET_HEADS, 1, RET_QK_DIM)
    wk = w_in[:, o + 256:o + 512].reshape(d, RET_HEADS, 1, RET_QK_DIM)
    dup = lambda w: jnp.broadcast_to(w, (d, RET_HEADS, 2, RET_QK_DIM)).reshape(d, TN)
    w_all = jnp.concatenate(
        [w_in[:, :DIL_COLS], dup(wq), dup(wk), w_in[:, o + 512:o + 1536],
         w_in[:, o + RET_COLS:o + RET_COLS + MEM_WIDTH]], axis=1).astype(BF16)

    ones = jnp.ones((TN,), F32)
    gv = []
    for g in range(3):
        gv += [jnp.tile(dil_q_g[g], DIL_HEADS) * (DIL_HEAD_DIM ** -0.5 * LOG2E), jnp.tile(dil_k_g[g], DIL_HEADS),
               ones]
    gv += [ones, ones * (RET_QK_DIM ** -0.5), ones, ones, jnp.tile(mem_q_g, MEM_HEADS) * (MEM_HEAD_DIM ** -0.5)]
    gv = jnp.stack(gv).reshape(N_TILES, 1, TN).astype(F32)
    bm = jnp.asarray(np.stack([_block_diag_ones(DIL_HEAD_DIM), _block_diag_ones(MEM_HEAD_DIM)]), BF16)
    kinds = tuple(tuple(sorted(set(TILE_EPILOGUE[sub::SUB]))) for sub in range(SUB))

    grid_spec = pltpu.PrefetchScalarGridSpec(
        num_scalar_prefetch=2,
        grid=(b, N_TILES // SUB),
        in_specs=[
            pl.BlockSpec((None, s, LANES), functools.partial(lambda i, n, *_, k: (i, 0, k), k=k))
            for k in range(d // LANES)
        ] + [
            pl.BlockSpec((1, d), lambda i, n, *_: (0, 0)),
            pl.BlockSpec((d, SUB * TN), lambda i, n, *_: (0, n)),
            pl.BlockSpec((SUB, 1, TN), lambda i, n, *_: (n, 0, 0)),
            pl.BlockSpec((2, MXU_DIM, MXU_DIM), lambda i, n, *_: (0, 0, 0)),
        ],
        out_specs=pl.BlockSpec((None, s, SUB * TN), lambda i, n, *_: (i, 0, n)),
        scratch_shapes=[pltpu.VMEM((3, s, d), BF16), pltpu.VMEM((d // LANES, s, LANES), F32)],
    )
    return pl.pallas_call(
        functools.partial(_inproj_kernel, kinds=kinds),
        grid_spec=grid_spec,
        out_shape=jax.ShapeDtypeStruct((b, s, N_TILES * TN), BF16),
        compiler_params=pltpu.CompilerParams(
            dimension_semantics=("arbitrary", "arbitrary"), vmem_limit_bytes=VMEM_LIMIT_BYTES),
        name="in_projection",
    )(jnp.asarray(TILE_EPILOGUE, jnp.int32), jnp.asarray(TILE_SLOT, jnp.int32), *([x] * (d // LANES)),
      norm1_g.reshape(1, d), w_all, gv, bm)


QB = 128
KW = 256


PAIRS = 2


def _dil_kernel(slope_ref, *refs):
    for j in range(PAIRS):
        lanes = pl.ds(j * LANES, LANES)
        hp = pl.program_id(1) * PAIRS + j
        _dil_pair(slope_ref[2 * hp], slope_ref[2 * hp + 1], [r.at[:, lanes] for r in refs[:9]],
                  refs[9].at[:, lanes], refs[10 + 6 * j:16 + 6 * j], refs[10 + 6 * PAIRS + j])


def _dil_pair(s0, s1, qkv, o_ref, state_refs, bias_s):
    state = (state_refs[:3], state_refs[3:])
    seq = o_ref.shape[0]

    lane = lax.broadcasted_iota(jnp.int32, (QB, LANES), 1)
    first = lane < DIL_HEAD_DIM
    ones = jnp.ones((KW, LANES), BF16)

    for g, (_, r) in enumerate(DIL_PATTERNS):
        q_ref, k_ref, v_ref = qkv[3 * g:3 * g + 3]
        n_sub = seq // r
        per = n_sub // QB
        kw = min(KW, n_sub)
        r_prev = DIL_PATTERNS[g - 1][1] if g else 1
        ratio = r // r_prev
        src = state[(g - 1) % 2]
        dst = state[g % 2]

        row = lax.broadcasted_iota(jnp.int32, (2 * QB, kw), 0)
        col = lax.broadcasted_iota(jnp.int32, (2 * QB, kw), 1)
        base = (col - (row & (QB - 1))).astype(F32)
        slope = jnp.where(row >= QB, s1, s0) * (float(r) * LOG2E)
        offsets = (0, -DIL_HALF, -2 * DIL_HALF) if kw == KW else (0,)
        for vi, off in enumerate(offsets):
            a = jnp.abs(base + float(off))
            bias_s[vi, :, :kw] = jnp.where(a <= float(DIL_HALF), -slope * a, MASK_VALUE)

        def block(t, carry, g=g, r_prev=r_prev, ratio=ratio, n_sub=n_sub, per=per, kw=kw, src=src, dst=dst,
                  q_ref=q_ref, k_ref=k_ref, v_ref=v_ref):
            cls = t // per
            qb = t % per
            rows = pl.ds(pl.multiple_of(t * QB, QB), QB)
            if kw == KW:
                ks = jnp.clip(qb * QB - DIL_HALF, 0, n_sub - KW)
                kabs = pl.multiple_of(cls * n_sub + ks, DIL_HALF)
                vi = jnp.where(qb == 0, 0, jnp.where(qb == per - 1, 2, 1))
            else:
                kabs = pl.multiple_of(t * QB, QB)
                vi = 0
            q = q_ref[rows, :]
            kwin = k_ref[pl.ds(kabs, kw), :]
            vwin = v_ref[pl.ds(kabs, kw), :]
            zero = jnp.zeros_like(q)
            q2 = jnp.concatenate([jnp.where(first, q, zero), jnp.where(first, zero, q)], axis=0)
            sc = lax.dot_general(q2, kwin, (((1,), (1,)), ((), ())), preferred_element_type=F32)
            sc = sc + bias_s[vi, :, :kw]
            m = jnp.max(sc, axis=1, keepdims=True)
            p = jnp.exp2(sc - m).astype(BF16)
            od = jnp.dot(p, jnp.concatenate([vwin, ones[:kw]], axis=1), preferred_element_type=F32)
            o = od[:, :LANES]
            dd = od[:, LANES:]
            accb = jnp.where(first, o[:QB], o[QB:])
            mb = jnp.where(first, m[:QB], m[QB:])
            db = jnp.where(first, dd[:QB], dd[QB:])
            if g == 0:
                dst[0][rows, :] = accb
                dst[1][rows, :] = mb
                dst[2][rows, :] = db
            else:
                n_sub_prev = seq // r_prev
                start = (cls // ratio) * n_sub_prev + cls % ratio + ratio * QB * qb
                prev = pl.ds(start, QB, stride=ratio)
                m_old = src[1][prev, :]
                m_new = jnp.maximum(m_old, mb)
                a_old = jnp.exp2(m_old - m_new)
                a_blk = jnp.exp2(mb - m_new)
                dst[0][rows, :] = a_old * src[0][prev, :] + a_blk * accb
                dst[2][rows, :] = a_old * src[2][prev, :] + a_blk * db
                if g + 1 < len(DIL_PATTERNS):
                    dst[1][rows, :] = m_new
            return carry

        lax.fori_loop(0, seq // QB, block, 0, unroll=True)

    last = len(DIL_PATTERNS) - 1
    r = DIL_PATTERNS[last][1]
    per = seq // r // QB
    acc_ref, _, den_ref = state[last % 2]
    nat_ref = state[(last + 1) % 2][0]

    def unpermute(t, c):
        rows = pl.ds(pl.multiple_of(t * QB, QB), QB)
        cls = t // per
        residue = 0
        for gg in range(last, 0, -1):
            ratio = DIL_PATTERNS[gg][1] // DIL_PATTERNS[gg - 1][1]
            residue = residue + DIL_PATTERNS[gg - 1][1] * (cls % ratio)
            cls = cls // ratio
        nat_ref[pl.ds(residue + r * QB * (t % per), QB, stride=r), :] = acc_ref[rows, :] / den_ref[rows, :]
        return c

    lax.fori_loop(0, seq // QB, unpermute, 0, unroll=2)
    rc = 256

    def fin(i, c):
        rows = pl.ds(pl.multiple_of(i * rc, rc), rc)
        o_ref[rows, :] = nat_ref[rows, :].astype(BF16)
        return c

    lax.fori_loop(0, seq // rc, fin, 0, unroll=2)


def _dilated_attention(proj):
    b, s, _ = proj.shape
    assert DIL_PATTERNS[0][1] == 1 and all(w // (2 * r) == DIL_HALF for w, r in DIL_PATTERNS)
    slopes = jnp.exp2(-jnp.arange(1, DIL_HEADS + 1, dtype=F32) * (8.0 / DIL_HEADS))
    steps = DIL_HEADS // 2 // PAIRS
    width = PAIRS * LANES
    in_specs = [
        pl.BlockSpec((None, s, width), functools.partial(lambda i, hp, *_, t: (i, 0, t * steps + hp), t=t))
        for t in range(9)
    ]
    grid_spec = pltpu.PrefetchScalarGridSpec(
        num_scalar_prefetch=1,
        grid=(b, steps),
        in_specs=in_specs,
        out_specs=pl.BlockSpec((None, s, width), lambda i, hp, *_: (i, 0, hp)),
        scratch_shapes=[pltpu.VMEM((s, LANES), F32)] * (6 * PAIRS) + [pltpu.VMEM((3, 2 * QB, KW), F32)] * PAIRS,
    )
    return pl.pallas_call(
        _dil_kernel,
        grid_spec=grid_spec,
        out_shape=jax.ShapeDtypeStruct((b, s, DIL_WIDTH), BF16),
        compiler_params=pltpu.CompilerParams(
            dimension_semantics=("arbitrary", "arbitrary"), vmem_limit_bytes=VMEM_LIMIT_BYTES),
        name="dilated_attention",
    )(slopes, *([proj] * 9))


def _ret_kernel(dl_ref, qq_ref, kk_ref, v_ref, sg_ref, gn_ref, o_ref, u_s, st_s):
    for h in range(RET_HEADS):
        lanes = pl.ds(h * LANES, LANES)
        _ret_head(h, dl_ref, qq_ref.at[:, lanes], kk_ref.at[:, lanes], v_ref.at[:, lanes], sg_ref.at[:, lanes],
                  gn_ref.at[:, lanes], o_ref.at[:, lanes], u_s.at[h], st_s.at[h])


def _ret_head(h, dl_ref, qq_ref, kk_ref, v_ref, sg_ref, gn_ref, o_ref, u_s, st_s):
    seq = o_ref.shape[0]
    c = RET_CHUNK
    n_chunks = seq // c
    dk = RET_QK_DIM

    def log_sigmoid(z):
        return jnp.minimum(z, 0.0) - jnp.log1p(jnp.exp(-jnp.abs(z)))

    lgf = log_sigmoid(jnp.full((1, LANES), dl_ref[h], F32))
    lgb = log_sigmoid(jnp.full((1, LANES), dl_ref[RET_HEADS + h], F32))
    ii = lax.broadcasted_iota(jnp.int32, (c, LANES), 0).astype(F32)
    jj = lax.broadcasted_iota(jnp.int32, (c, LANES), 1).astype(F32)
    first = lax.broadcasted_iota(jnp.int32, (c, LANES), 1) < dk
    diff = ii - jj
    decay = jnp.exp(jnp.where(diff >= 0.0, lgf * diff, -lgb * diff))
    xi = jnp.exp(jnp.where(first, lgf * (ii + 1.0), lgb * (float(c) - ii)))
    zeta = jnp.exp(jnp.where(first, lgf * (float(c - 1) - ii), lgb * ii))
    chunk_f = jnp.exp(lgf * float(c))
    chunk_b = jnp.exp(lgb * float(c))

    def rows_of(n):
        return pl.ds(pl.multiple_of(n * c, c), c)

    def incr(n, carry):
        kz = (kk_ref[rows_of(n), :].astype(F32) * zeta).T.astype(BF16)
        u_s[n] = jnp.dot(kz, v_ref[rows_of(n), :], preferred_element_type=F32)
        return carry

    lax.fori_loop(0, n_chunks, incr, 0, unroll=True)

    def scan_f(n, state):
        st_s[n, :dk, :] = state.astype(BF16)
        return state * chunk_f + u_s[n, :dk, :]

    lax.fori_loop(0, n_chunks, scan_f, jnp.zeros((dk, LANES), F32))

    def scan_b(i, state):
        n = n_chunks - 1 - i
        st_s[n, dk:, :] = state.astype(BF16)
        return state * chunk_b + u_s[n, dk:, :]

    lax.fori_loop(0, n_chunks, scan_b, jnp.zeros((dk, LANES), F32))

    gn = gn_ref[...]

    def out(n, carry):
        rows = rows_of(n)
        qq = qq_ref[rows, :]
        kk = kk_ref[rows, :]
        q0 = jnp.where(first, qq, jnp.zeros_like(qq))
        a = lax.dot_general(q0, kk, (((1,), (1,)), ((), ())), preferred_element_type=F32)
        p = (a * decay).astype(BF16)
        qx = (qq.astype(F32) * xi).astype(BF16)
        y = (jnp.dot(p, v_ref[rows, :], preferred_element_type=F32)
             + jnp.dot(qx, st_s[n], preferred_element_type=F32))
        mu = jnp.mean(y, axis=-1, keepdims=True)
        yc = y - mu
        var = jnp.mean(yc * yc, axis=-1, keepdims=True)
        yn = yc * lax.rsqrt(var + NORM_EPS)
        o_ref[rows, :] = (sg_ref[rows, :].astype(F32) * (yn * gn)).astype(BF16)
        return carry

    lax.fori_loop(0, n_chunks, out, 0, unroll=True)


def _retention(proj, decay_logit, gn_g):
    b, s, _ = proj.shape
    n_chunks = s // RET_CHUNK

    def col(tile):
        return lambda i, *_: (i, 0, tile)

    width = RET_HEADS * RET_V_DIM
    assert width == TN
    grid_spec = pltpu.PrefetchScalarGridSpec(
        num_scalar_prefetch=1,
        grid=(b,),
        in_specs=[pl.BlockSpec((None, s, TN), col(t)) for t in (TILE_RET_Q, TILE_RET_K, TILE_RET_V, TILE_RET_G)]
        + [pl.BlockSpec((1, width), lambda i, *_: (0, 0))],
        out_specs=pl.BlockSpec((None, s, width), lambda i, *_: (i, 0, 0)),
        scratch_shapes=[pltpu.VMEM((RET_HEADS, n_chunks, 2 * RET_QK_DIM, RET_V_DIM), F32),
                        pltpu.VMEM((RET_HEADS, n_chunks, 2 * RET_QK_DIM, RET_V_DIM), BF16)],
    )
    return pl.pallas_call(
        _ret_kernel,
        grid_spec=grid_spec,
        out_shape=jax.ShapeDtypeStruct((b, s, width), BF16),
        compiler_params=pltpu.CompilerParams(
            dimension_semantics=("arbitrary",), vmem_limit_bytes=VMEM_LIMIT_BYTES),
        name="retention",
    )(decay_logit.astype(F32).reshape(-1), proj, proj, proj, proj, gn_g.reshape(1, -1))


def _mem_kernel(mem_ref, mg_ref, wkv_ref, kg_ref, q_ref, o_ref, k_s, v_s):
    seq = o_ref.shape[0]
    hm = _rms(mem_ref[...], mg_ref[...]).astype(BF16)
    kv = jnp.dot(hm, wkv_ref[...], preferred_element_type=F32)
    kg = kg_ref[...]
    for h in range(MEM_HEADS):
        c0 = h * MEM_HEAD_DIM
        k_s[h] = _rms(kv[:, c0:c0 + MEM_HEAD_DIM], kg).astype(BF16)
        v_s[h] = kv[:, MEM_WIDTH + c0:MEM_WIDTH + c0 + MEM_HEAD_DIM].astype(BF16)

    rb = 256

    def block(i, carry):
        rows = pl.ds(pl.multiple_of(i * rb, rb), rb)
        for h in range(MEM_HEADS):
            c0 = h * MEM_HEAD_DIM
            q = q_ref[rows, c0:c0 + MEM_HEAD_DIM]
            sc = lax.dot_general(q, k_s[h], (((1,), (1,)), ((), ())), preferred_element_type=F32)
            m = jnp.max(sc, axis=1, keepdims=True)
            p = jnp.exp(sc - m)
            dsum = jnp.sum(p, axis=1, keepdims=True)
            o = jnp.dot(p.astype(BF16), v_s[h], preferred_element_type=F32)
            o_ref[rows, c0:c0 + MEM_HEAD_DIM] = (o / dsum).astype(BF16)
        return carry

    lax.fori_loop(0, seq // rb, block, 0, unroll=True)


def _memory_attention(proj, mem, mem_norm_g, w_mem_kv, mem_k_g):
    b, s, _ = proj.shape
    m, d = mem.shape[1:]
    return pl.pallas_call(
        _mem_kernel,
        grid=(b,),
        in_specs=[
            pl.BlockSpec((None, m, d), lambda i: (i, 0, 0)),
            pl.BlockSpec((1, d), lambda i: (0, 0)),
            pl.BlockSpec((d, 2 * MEM_WIDTH), lambda i: (0, 0)),
            pl.BlockSpec((1, MEM_HEAD_DIM), lambda i: (0, 0)),
            pl.BlockSpec((None, s, TN), lambda i: (i, 0, TILE_MEMQ)),
        ],
        out_specs=pl.BlockSpec((None, s, MEM_WIDTH), lambda i: (i, 0, 0)),
        out_shape=jax.ShapeDtypeStruct((b, s, MEM_WIDTH), BF16),
        scratch_shapes=[pltpu.VMEM((MEM_HEADS, m, MEM_HEAD_DIM), BF16), pltpu.VMEM((MEM_HEADS, m, MEM_HEAD_DIM), BF16)],
        compiler_params=pltpu.CompilerParams(
            dimension_semantics=("arbitrary",), vmem_limit_bytes=VMEM_LIMIT_BYTES),
        name="memory_attention",
    )(mem, mem_norm_g.reshape(1, d), w_mem_kv.astype(BF16), mem_k_g.reshape(1, -1), proj)


TM_MERGE = 1024


def _merge_kernel(x_ref, g1_ref, wg_ref, yd_ref, yr_ref, ym_ref, wd_ref, wr_ref, wm_ref, wo_ref, g2_ref,
                  x1_ref, h2_ref):
    d = x_ref.shape[1]
    x = x_ref[...]
    h = _rms(x, g1_ref[...]).astype(BF16)
    merged = None
    for j, (y_ref, w_ref) in enumerate(((yd_ref, wd_ref), (yr_ref, wr_ref), (ym_ref, wm_ref))):
        gate = jax.nn.sigmoid(jnp.dot(h, wg_ref[:, j * d:(j + 1) * d], preferred_element_type=F32))
        term = gate * jnp.dot(y_ref[...], w_ref[...], preferred_element_type=F32)
        merged = term if merged is None else merged + term
    x1 = x + jnp.dot(merged.astype(BF16), wo_ref[...], preferred_element_type=F32)
    x1_ref[...] = x1
    h2_ref[...] = _rms(x1, g2_ref[...]).astype(BF16)


def _merge_out(x, norm1_g, w_gate, y_dil, y_ret, y_mem, w_bd, w_br, w_bm, w_out, norm2_g):
    b, s, d = x.shape
    t = b * s
    tm = TM_MERGE
    row = lambda w: pl.BlockSpec((tm, w), lambda i: (i, 0))
    full = lambda r, c: pl.BlockSpec((r, c), lambda i: (0, 0), pipeline_mode=pl.Buffered(1))
    x1, h2 = pl.pallas_call(
        _merge_kernel,
        grid=(t // tm,),
        in_specs=[row(d), full(1, d), full(d, 3 * d), row(DIL_WIDTH), row(RET_HEADS * RET_V_DIM), row(MEM_WIDTH),
                  full(DIL_WIDTH, d), full(RET_HEADS * RET_V_DIM, d), full(MEM_WIDTH, d), full(d, d), full(1, d)],
        out_specs=[row(d), row(d)],
        out_shape=[jax.ShapeDtypeStruct((t, d), F32), jax.ShapeDtypeStruct((t, d), BF16)],
        compiler_params=pltpu.CompilerParams(
            dimension_semantics=("arbitrary",), vmem_limit_bytes=VMEM_LIMIT_BYTES),
        name="merge_out_projection",
    )(x.reshape(t, d), norm1_g.reshape(1, d), w_gate.astype(BF16), y_dil.reshape(t, -1), y_ret.reshape(t, -1),
      y_mem.reshape(t, -1), w_bd.astype(BF16), w_br.astype(BF16), w_bm.astype(BF16), w_out.astype(BF16),
      norm2_g.reshape(1, d))
    return x1.reshape(b, s, d), h2.reshape(b, s, d)


TM_FFN = 1024
HALO = 16
FC = 256


def _ffn_kernel(h2_ref, hp_ref, hn_ref, x1_ref, wi_ref, cw_ref, cb_ref, wo_ref, o_ref, y_s):
    i = pl.program_id(1)
    tm = h2_ref.shape[0]
    d_ff = wo_ref.shape[0]
    h = h2_ref[...]
    prev = jnp.where(i == 0, jnp.zeros_like(hp_ref[...]), hp_ref[...])
    nxt = jnp.where(i == pl.num_programs(1) - 1, jnp.zeros_like(hn_ref[...]), hn_ref[...])
    hx = jnp.concatenate([prev, h, nxt], axis=0)
    rows = tm + 2 * HALO
    for c0 in range(0, d_ff, FC):
        u = jnp.dot(hx, wi_ref[:, c0:c0 + FC], preferred_element_type=F32)
        gt = jnp.dot(h, wi_ref[:, d_ff + c0:d_ff + c0 + FC], preferred_element_type=F32)
        cw = cw_ref[:, c0:c0 + FC]
        u_prev = pltpu.roll(u, 1, axis=0)[HALO:HALO + tm]
        u_next = pltpu.roll(u, rows - 1, axis=0)[HALO:HALO + tm]
        c = cb_ref[:, c0:c0 + FC] + u_prev * cw[0:1] + u[HALO:HALO + tm] * cw[1:2] + u_next * cw[2:3]
        y = (0.5 * c * (1.0 + lax.erf(c * (2.0 ** -0.5)))) * gt
        y_s[:, c0:c0 + FC] = y.astype(BF16)
    o_ref[...] = x1_ref[...] + jnp.dot(y_s[...], wo_ref[...], preferred_element_type=F32)


def _ffn(h2, x1, w_ffn_in, conv_w, conv_b, w_ffn_out):
    b, s, d = x1.shape
    tm = TM_FFN
    nt = s // tm
    hb = tm // HALO
    const = dict(pipeline_mode=pl.Buffered(1))
    return pl.pallas_call(
        _ffn_kernel,
        grid=(b, nt),
        in_specs=[
            pl.BlockSpec((None, tm, d), lambda j, i: (j, i, 0)),
            pl.BlockSpec((None, HALO, d), lambda j, i: (j, jnp.maximum(i * hb - 1, 0), 0)),
            pl.BlockSpec((None, HALO, d), lambda j, i: (j, jnp.minimum((i + 1) * hb, s // HALO - 1), 0)),
            pl.BlockSpec((None, tm, d), lambda j, i: (j, i, 0)),
            pl.BlockSpec((d, 2 * D_FF), lambda j, i: (0, 0), **const),
            pl.BlockSpec((3, D_FF), lambda j, i: (0, 0), **const),
            pl.BlockSpec((1, D_FF), lambda j, i: (0, 0), **const),
            pl.BlockSpec((D_FF, d), lambda j, i: (0, 0), **const),
        ],
        out_specs=pl.BlockSpec((None, tm, d), lambda j, i: (j, i, 0)),
        out_shape=jax.ShapeDtypeStruct((b, s, d), F32),
        scratch_shapes=[pltpu.VMEM((tm, D_FF), BF16)],
        compiler_params=pltpu.CompilerParams(
            dimension_semantics=("arbitrary", "arbitrary"), vmem_limit_bytes=VMEM_LIMIT_BYTES),
        name="conv_glu_ffn",
    )(h2, h2, h2, x1, w_ffn_in.astype(BF16), conv_w, conv_b.reshape(1, -1), w_ffn_out.astype(BF16))


def kernel(x, mem, norm1_g, w_in, dil_q_norm_g, dil_k_norm_g, ret_decay_logit, ret_gn_g, mem_norm_g, w_mem_kv,
           mem_q_norm_g, mem_k_norm_g, w_branch_dil, w_branch_ret, w_branch_mem, w_out, norm2_g, w_ffn_in,
           ffn_conv_w, ffn_conv_b, w_ffn_out):
    depth = w_in.shape[0]
    gate0 = DIL_COLS + RET_COLS + MEM_WIDTH
    for l in range(depth):
        proj = _in_projection(x, norm1_g[l], w_in[l], dil_q_norm_g[l], dil_k_norm_g[l], mem_q_norm_g[l])
        y_dil = _dilated_attention(proj)
        y_ret = _retention(proj, ret_decay_logit[l], ret_gn_g[l])
        y_mem = _memory_attention(proj, mem, mem_norm_g[l], w_mem_kv[l], mem_k_norm_g[l])
        x1, h2 = _merge_out(x, norm1_g[l], w_in[l][:, gate0:], y_dil, y_ret, y_mem, w_branch_dil[l],
                            w_branch_ret[l], w_branch_mem[l], w_out[l], norm2_g[l])
        x = _ffn(h2, x1, w_ffn_in[l], ffn_conv_w[l], ffn_conv_b[l], w_ffn_out[l])
    return x
```

```python
import functools

import numpy as np
import jax
import jax.numpy as jnp
from jax import lax
from jax.experimental import pallas as pl
from jax.experimental.pallas import tpu as pltpu

F32 = jnp.float32
BF16 = jnp.bfloat16

D_MODEL = 1024
SEQ = 2048
N_MEM = 256
DIL_PATTERNS = ((128, 1), (512, 4), (2048, 16))
DIL_HEADS = 8
DIL_HEAD_DIM = 64
DIL_WIDTH = DIL_HEADS * DIL_HEAD_DIM
DIL_HALF = 64
RET_HEADS = 4
RET_QK_DIM = 64
RET_V_DIM = 128
RET_CHUNK = 128
MEM_HEADS = 4
MEM_HEAD_DIM = 128
MEM_WIDTH = MEM_HEADS * MEM_HEAD_DIM
D_FF = 2816
NORM_EPS = 1e-6
MASK_VALUE = -1e30
LOG2E = 1.4426950408889634
DIL_COLS = 3 * 3 * DIL_WIDTH
RET_COLS = 2 * RET_HEADS * RET_QK_DIM + 2 * RET_HEADS * RET_V_DIM

LANES = 128
MXU_DIM = 256
VMEM_LIMIT_BYTES = 56 * 1024 * 1024

TN = 512
SUB = 2
EPI_SCALE, EPI_NORM64, EPI_NORM128, EPI_SILU = 0, 1, 2, 3
N_TILES = 14
TILE_RET_Q, TILE_RET_K, TILE_RET_V, TILE_RET_G, TILE_MEMQ = 9, 10, 11, 12, 13
TILE_EPILOGUE = (EPI_NORM64, EPI_NORM64, EPI_SCALE) * 3 + (EPI_SCALE,) * 3 + (EPI_SILU, EPI_NORM128)
TILE_SLOT = (0, 0, 0, 1, 1, 1, 2, 2, 2) + (0,) * 5


def _rms(xf, g):
    ms = jnp.mean(xf * xf, axis=-1, keepdims=True)
    return xf * lax.rsqrt(ms + NORM_EPS) * g


def _inproj_kernel(kind_ref, slot_ref, *refs, kinds):
    n_slabs = len(refs) - 7
    xs = refs[:n_slabs]
    g1_ref, w_ref, gv_ref, bm_ref, o_ref, hs_ref, hp_ref = refs[n_slabs:]
    n = pl.program_id(1)
    seq = xs[0].shape[0]

    @pl.when(n == 0)
    def _prologue():
        g = g1_ref[...]
        rc = 256

        def normed(rows):
            return _rms(jnp.concatenate([x[rows, :] for x in xs], axis=1), g)

        def nat(i, c):
            rows = pl.ds(pl.multiple_of(i * rc, rc), rc)
            hs_ref[0, rows, :] = normed(rows).astype(BF16)
            return c

        lax.fori_loop(0, seq // rc, nat, 0, unroll=2)

        for slot in range(1, len(DIL_PATTERNS)):
            r_prev, r = DIL_PATTERNS[slot - 1][1], DIL_PATTERNS[slot][1]
            ratio = r // r_prev
            n_sub_prev, n_sub = seq // r_prev, seq // r
            rows = min(rc, n_sub)
            per = n_sub // rows
            last = slot + 1 == len(DIL_PATTERNS)

            def stage(i, c, slot=slot, ratio=ratio, n_sub_prev=n_sub_prev, n_sub=n_sub, rows=rows, per=per,
                      last=last):
                cls = i // per
                j = i % per
                src = pl.ds((cls // ratio) * n_sub_prev + cls % ratio + ratio * rows * j, rows, stride=ratio)
                dst = pl.ds(pl.multiple_of(cls * n_sub + j * rows, rows), rows)
                if slot == 1:
                    hn = normed(src)
                else:
                    hn = jnp.concatenate([hp_ref[k, src, :] for k in range(n_slabs)], axis=1)
                hs_ref[slot, dst, :] = hn.astype(BF16)
                if not last:
                    for k in range(n_slabs):
                        hp_ref[k, dst, :] = hn[:, k * LANES:(k + 1) * LANES]
                return c

            lax.fori_loop(0, r * per, stage, 0, unroll=2)

    def head_norm(acc, gv, which, head_dim):
        sq = (acc * acc).astype(BF16)
        ms = jnp.concatenate(
            [jnp.dot(sq[:, c0:c0 + MXU_DIM], bm_ref[which], preferred_element_type=F32)
             for c0 in range(0, TN, MXU_DIM)], axis=1)
        return acc * lax.rsqrt(ms * (1.0 / head_dim) + NORM_EPS) * gv

    epilogues = {
        EPI_SCALE: lambda acc, gv: acc * gv,
        EPI_NORM64: lambda acc, gv: head_norm(acc, gv, 0, DIL_HEAD_DIM),
        EPI_NORM128: lambda acc, gv: head_norm(acc, gv, 1, MEM_HEAD_DIM),
        EPI_SILU: lambda acc, gv: acc * jax.nn.sigmoid(acc),
    }
    rm = 512
    for sub in range(SUB):
        kind = kind_ref[n * SUB + sub]
        slot = slot_ref[n * SUB + sub]
        cols = slice(sub * TN, (sub + 1) * TN)
        for epi in kinds[sub]:

            @pl.when(kind == epi)
            def _(epi=epi, slot=slot, cols=cols, sub=sub):
                gv = gv_ref[sub]
                for r0 in range(0, seq, rm):
                    acc = jnp.dot(hs_ref[slot, r0:r0 + rm, :], w_ref[:, cols], preferred_element_type=F32)
                    o_ref[r0:r0 + rm, cols] = epilogues[epi](acc, gv).astype(BF16)


def _block_diag_ones(block):
    i = np.arange(MXU_DIM)
    return (i[:, None] // block == i[None, :] // block).astype(np.float32)


def _in_projection(x, norm1_g, w_in, dil_q_g, dil_k_g, mem_q_g):
    b, s, d = x.shape
    assert len(DIL_PATTERNS) == 3
    o = DIL_COLS
    wq = w_in[:, o:o + 256].reshape(d, RET_HEADS, 1, RET_QK_DIM)
    wk = w_in[:, o + 256:o + 512].reshape(d, RET_HEADS, 1, RET_QK_DIM)
    dup = lambda w: jnp.broadcast_to(w, (d, RET_HEADS, 2, RET_QK_DIM)).reshape(d, TN)
    w_all = jnp.concatenate(
        [w_in[:, :DIL_COLS], dup(wq), dup(wk), w_in[:, o + 512:o + 1536],
         w_in[:, o + RET_COLS:o + RET_COLS + MEM_WIDTH]], axis=1).astype(BF16)

    ones = jnp.ones((TN,), F32)
    gv = []
    for g in range(3):
        gv += [jnp.tile(dil_q_g[g], DIL_HEADS) * (DIL_HEAD_DIM ** -0.5 * LOG2E), jnp.tile(dil_k_g[g], DIL_HEADS),
               ones]
    gv += [ones, ones * (RET_QK_DIM ** -0.5), ones, ones, jnp.tile(mem_q_g, MEM_HEADS) * (MEM_HEAD_DIM ** -0.5)]
    gv = jnp.stack(gv).reshape(N_TILES, 1, TN).astype(F32)
    bm = jnp.asarray(np.stack([_block_diag_ones(DIL_HEAD_DIM), _block_diag_ones(MEM_HEAD_DIM)]), BF16)
    kinds = tuple(tuple(sorted(set(TILE_EPILOGUE[sub::SUB]))) for sub in range(SUB))

    grid_spec = pltpu.PrefetchScalarGridSpec(
        num_scalar_prefetch=2,
        grid=(b, N_TILES // SUB),
        in_specs=[
            pl.BlockSpec((None, s, LANES), functools.partial(lambda i, n, *_, k: (i, 0, k), k=k))
            for k in range(d // LANES)
        ] + [
            pl.BlockSpec((1, d), lambda i, n, *_: (0, 0)),
            pl.BlockSpec((d, SUB * TN), lambda i, n, *_: (0, n)),
            pl.BlockSpec((SUB, 1, TN), lambda i, n, *_: (n, 0, 0)),
            pl.BlockSpec((2, MXU_DIM, MXU_DIM), lambda i, n, *_: (0, 0, 0)),
        ],
        out_specs=pl.BlockSpec((None, s, SUB * TN), lambda i, n, *_: (i, 0, n)),
        scratch_shapes=[pltpu.VMEM((3, s, d), BF16), pltpu.VMEM((d // LANES, s, LANES), F32)],
    )
    return pl.pallas_call(
        functools.partial(_inproj_kernel, kinds=kinds),
        grid_spec=grid_spec,
        out_shape=jax.ShapeDtypeStruct((b, s, N_TILES * TN), BF16),
        compiler_params=pltpu.CompilerParams(
            dimension_semantics=("arbitrary", "arbitrary"), vmem_limit_bytes=VMEM_LIMIT_BYTES),
        name="in_projection",
    )(jnp.asarray(TILE_EPILOGUE, jnp.int32), jnp.asarray(TILE_SLOT, jnp.int32), *([x] * (d // LANES)),
      norm1_g.reshape(1, d), w_all, gv, bm)


QB = 128
KW = 256


PAIRS = 2


def _dil_kernel(slope_ref, *refs):
    for j in range(PAIRS):
        lanes = pl.ds(j * LANES, LANES)
        hp = pl.program_id(1) * PAIRS + j
        _dil_pair(slope_ref[2 * hp], slope_ref[2 * hp + 1], [r.at[:, lanes] for r in refs[:9]],
                  refs[9].at[:, lanes], refs[10 + 6 * j:16 + 6 * j], refs[10 + 6 * PAIRS + j])


def _dil_pair(s0, s1, qkv, o_ref, state_refs, bias_s):
    state = (state_refs[:3], state_refs[3:])
    seq = o_ref.shape[0]

    lane = lax.broadcasted_iota(jnp.int32, (QB, LANES), 1)
    first = lane < DIL_HEAD_DIM
    ones = jnp.ones((KW, LANES), BF16)

    for g, (_, r) in enumerate(DIL_PATTERNS):
        q_ref, k_ref, v_ref = qkv[3 * g:3 * g + 3]
        n_sub = seq // r
        per = n_sub // QB
        kw = min(KW, n_sub)
        r_prev = DIL_PATTERNS[g - 1][1] if g else 1
        ratio = r // r_prev
        src = state[(g - 1) % 2]
        dst = state[g % 2]

        row = lax.broadcasted_iota(jnp.int32, (2 * QB, kw), 0)
        col = lax.broadcasted_iota(jnp.int32, (2 * QB, kw), 1)
        base = (col - (row & (QB - 1))).astype(F32)
        slope = jnp.where(row >= QB, s1, s0) * (float(r) * LOG2E)
        offsets = (0, -DIL_HALF, -2 * DIL_HALF) if kw == KW else (0,)
        for vi, off in enumerate(offsets):
            a = jnp.abs(base + float(off))
            bias_s[vi, :, :kw] = jnp.where(a <= float(DIL_HALF), -slope * a, MASK_VALUE)

        def block(t, carry, g=g, r_prev=r_prev, ratio=ratio, n_sub=n_sub, per=per, kw=kw, src=src, dst=dst,
                  q_ref=q_ref, k_ref=k_ref, v_ref=v_ref):
            cls = t // per
            qb = t % per
            rows = pl.ds(pl.multiple_of(t * QB, QB), QB)
            if kw == KW:
                ks = jnp.clip(qb * QB - DIL_HALF, 0, n_sub - KW)
                kabs = pl.multiple_of(cls * n_sub + ks, DIL_HALF)
                vi = jnp.where(qb == 0, 0, jnp.where(qb == per - 1, 2, 1))
            else:
                kabs = pl.multiple_of(t * QB, QB)
                vi = 0
            q = q_ref[rows, :]
            kwin = k_ref[pl.ds(kabs, kw), :]
            vwin = v_ref[pl.ds(kabs, kw), :]
            zero = jnp.zeros_like(q)
            q2 = jnp.concatenate([jnp.where(first, q, zero), jnp.where(first, zero, q)], axis=0)
            sc = lax.dot_general(q2, kwin, (((1,), (1,)), ((), ())), preferred_element_type=F32)
            sc = sc + bias_s[vi, :, :kw]
            m = jnp.max(sc, axis=1, keepdims=True)
            p = jnp.exp2(sc - m).astype(BF16)
            od = jnp.dot(p, jnp.concatenate([vwin, ones[:kw]], axis=1), preferred_element_type=F32)
            o = od[:, :LANES]
            dd = od[:, LANES:]
            accb = jnp.where(first, o[:QB], o[QB:])
            mb = jnp.where(first, m[:QB], m[QB:])
            db = jnp.where(first, dd[:QB], dd[QB:])
            if g == 0:
                dst[0][rows, :] = accb
                dst[1][rows, :] = mb
                dst[2][rows, :] = db
            else:
                n_sub_prev = seq // r_prev
                start = (cls // ratio) * n_sub_prev + cls % ratio + ratio * QB * qb
                prev = pl.ds(start, QB, stride=ratio)
                m_old = src[1][prev, :]
                m_new = jnp.maximum(m_old, mb)
                a_old = jnp.exp2(m_old - m_new)
                a_blk = jnp.exp2(mb - m_new)
                dst[0][rows, :] = a_old * src[0][prev, :] + a_blk * accb
                dst[2][rows, :] = a_old * src[2][prev, :] + a_blk * db
                if g + 1 < len(DIL_PATTERNS):
                    dst[1][rows, :] = m_new
            return carry

        lax.fori_loop(0, seq // QB, block, 0, unroll=True)

    last = len(DIL_PATTERNS) - 1
    r = DIL_PATTERNS[last][1]
    per = seq // r // QB
    acc_ref, _, den_ref = state[last % 2]
    nat_ref = state[(last + 1) % 2][0]

    def unpermute(t, c):
        rows = pl.ds(pl.multiple_of(t * QB, QB), QB)
        cls = t // per
        residue = 0
        for gg in range(last, 0, -1):
            ratio = DIL_PATTERNS[gg][1] // DIL_PATTERNS[gg - 1][1]
            residue = residue + DIL_PATTERNS[gg - 1][1] * (cls % ratio)
            cls = cls // ratio
        nat_ref[pl.ds(residue + r * QB * (t % per), QB, stride=r), :] = acc_ref[rows, :] / den_ref[rows, :]
        return c

    lax.fori_loop(0, seq // QB, unpermute, 0, unroll=2)
    rc = 256

    def fin(i, c):
        rows = pl.ds(pl.multiple_of(i * rc, rc), rc)
        o_ref[rows, :] = nat_ref[rows, :].astype(BF16)
        return c

    lax.fori_loop(0, seq // rc, fin, 0, unroll=2)


def _dilated_attention(proj):
    b, s, _ = proj.shape
    assert DIL_PATTERNS[0][1] == 1 and all(w // (2 * r) == DIL_HALF for w, r in DIL_PATTERNS)
    slopes = jnp.exp2(-jnp.arange(1, DIL_HEADS + 1, dtype=F32) * (8.0 / DIL_HEADS))
    steps = DIL_HEADS // 2 // PAIRS
    width = PAIRS * LANES
    in_specs = [
        pl.BlockSpec((None, s, width), functools.partial(lambda i, hp, *_, t: (i, 0, t * steps + hp), t=t))
        for t in range(9)
    ]
    grid_spec = pltpu.PrefetchScalarGridSpec(
        num_scalar_prefetch=1,
        grid=(b, steps),
        in_specs=in_specs,
        out_specs=pl.BlockSpec((None, s, width), lambda i, hp, *_: (i, 0, hp)),
        scratch_shapes=[pltpu.VMEM((s, LANES), F32)] * (6 * PAIRS) + [pltpu.VMEM((3, 2 * QB, KW), F32)] * PAIRS,
    )
    return pl.pallas_call(
        _dil_kernel,
        grid_spec=grid_spec,
        out_shape=jax.ShapeDtypeStruct((b, s, DIL_WIDTH), BF16),
        compiler_params=pltpu.CompilerParams(
            dimension_semantics=("arbitrary", "arbitrary"), vmem_limit_bytes=VMEM_LIMIT_BYTES),
        name="dilated_attention",
    )(slopes, *([proj] * 9))


def _ret_kernel(dl_ref, qq_ref, kk_ref, v_ref, sg_ref, gn_ref, o_ref, u_s, st_s):
    for h in range(RET_HEADS):
        lanes = pl.ds(h * LANES, LANES)
        _ret_head(h, dl_ref, qq_ref.at[:, lanes], kk_ref.at[:, lanes], v_ref.at[:, lanes], sg_ref.at[:, lanes],
                  gn_ref.at[:, lanes], o_ref.at[:, lanes], u_s.at[h], st_s.at[h])


def _ret_head(h, dl_ref, qq_ref, kk_ref, v_ref, sg_ref, gn_ref, o_ref, u_s, st_s):
    seq = o_ref.shape[0]
    c = RET_CHUNK
    n_chunks = seq // c
    dk = RET_QK_DIM

    def log_sigmoid(z):
        return jnp.minimum(z, 0.0) - jnp.log1p(jnp.exp(-jnp.abs(z)))

    lgf = log_sigmoid(jnp.full((1, LANES), dl_ref[h], F32))
    lgb = log_sigmoid(jnp.full((1, LANES), dl_ref[RET_HEADS + h], F32))
    ii = lax.broadcasted_iota(jnp.int32, (c, LANES), 0).astype(F32)
    jj = lax.broadcasted_iota(jnp.int32, (c, LANES), 1).astype(F32)
    first = lax.broadcasted_iota(jnp.int32, (c, LANES), 1) < dk
    diff = ii - jj
    decay = jnp.exp(jnp.where(diff >= 0.0, lgf * diff, -lgb * diff))
    xi = jnp.exp(jnp.where(first, lgf * (ii + 1.0), lgb * (float(c) - ii)))
    zeta = jnp.exp(jnp.where(first, lgf * (float(c - 1) - ii), lgb * ii))
    chunk_f = jnp.exp(lgf * float(c))
    chunk_b = jnp.exp(lgb * float(c))

    def rows_of(n):
        return pl.ds(pl.multiple_of(n * c, c), c)

    def incr(n, carry):
        kz = (kk_ref[rows_of(n), :].astype(F32) * zeta).T.astype(BF16)
        u_s[n] = jnp.dot(kz, v_ref[rows_of(n), :], preferred_element_type=F32)
        return carry

    lax.fori_loop(0, n_chunks, incr, 0, unroll=True)

    def scan_f(n, state):
        st_s[n, :dk, :] = state.astype(BF16)
        return state * chunk_f + u_s[n, :dk, :]

    lax.fori_loop(0, n_chunks, scan_f, jnp.zeros((dk, LANES), F32))

    def scan_b(i, state):
        n = n_chunks - 1 - i
        st_s[n, dk:, :] = state.astype(BF16)
        return state * chunk_b + u_s[n, dk:, :]

    lax.fori_loop(0, n_chunks, scan_b, jnp.zeros((dk, LANES), F32))

    gn = gn_ref[...]

    def out(n, carry):
        rows = rows_of(n)
        qq = qq_ref[rows, :]
        kk = kk_ref[rows, :]
        q0 = jnp.where(first, qq, jnp.zeros_like(qq))
        a = lax.dot_general(q0, kk, (((1,), (1,)), ((), ())), preferred_element_type=F32)
        p = (a * decay).astype(BF16)
        qx = (qq.astype(F32) * xi).astype(BF16)
        y = (jnp.dot(p, v_ref[rows, :], preferred_element_type=F32)
             + jnp.dot(qx, st_s[n], preferred_element_type=F32))
        mu = jnp.mean(y, axis=-1, keepdims=True)
        yc = y - mu
        var = jnp.mean(yc * yc, axis=-1, keepdims=True)
        yn = yc * lax.rsqrt(var + NORM_EPS)
        o_ref[rows, :] = (sg_ref[rows, :].astype(F32) * (yn * gn)).astype(BF16)
        return carry

    lax.fori_loop(0, n_chunks, out, 0, unroll=True)


def _retention(proj, decay_logit, gn_g):
    b, s, _ = proj.shape
    n_chunks = s // RET_CHUNK

    def col(tile):
        return lambda i, *_: (i, 0, tile)

    width = RET_HEADS * RET_V_DIM
    assert width == TN
    grid_spec = pltpu.PrefetchScalarGridSpec(
        num_scalar_prefetch=1,
        grid=(b,),
        in_specs=[pl.BlockSpec((None, s, TN), col(t)) for t in (TILE_RET_Q, TILE_RET_K, TILE_RET_V, TILE_RET_G)]
        + [pl.BlockSpec((1, width), lambda i, *_: (0, 0))],
        out_specs=pl.BlockSpec((None, s, width), lambda i, *_: (i, 0, 0)),
        scratch_shapes=[pltpu.VMEM((RET_HEADS, n_chunks, 2 * RET_QK_DIM, RET_V_DIM), F32),
                        pltpu.VMEM((RET_HEADS, n_chunks, 2 * RET_QK_DIM, RET_V_DIM), BF16)],
    )
    return pl.pallas_call(
        _ret_kernel,
        grid_spec=grid_spec,
        out_shape=jax.ShapeDtypeStruct((b, s, width), BF16),
        compiler_params=pltpu.CompilerParams(
            dimension_semantics=("arbitrary",), vmem_limit_bytes=VMEM_LIMIT_BYTES),
        name="retention",
    )(decay_logit.astype(F32).reshape(-1), proj, proj, proj, proj, gn_g.reshape(1, -1))


def _mem_kernel(mem_ref, mg_ref, wkv_ref, kg_ref, q_ref, o_ref, k_s, v_s):
    seq = o_ref.shape[0]
    hm = _rms(mem_ref[...], mg_ref[...]).astype(BF16)
    kv = jnp.dot(hm, wkv_ref[...], preferred_element_type=F32)
    kg = kg_ref[...]
    for h in range(MEM_HEADS):
        c0 = h * MEM_HEAD_DIM
        k_s[h] = _rms(kv[:, c0:c0 + MEM_HEAD_DIM], kg).astype(BF16)
        v_s[h] = kv[:, MEM_WIDTH + c0:MEM_WIDTH + c0 + MEM_HEAD_DIM].astype(BF16)

    rb = 256

    def block(i, carry):
        rows = pl.ds(pl.multiple_of(i * rb, rb), rb)
        for h in range(MEM_HEADS):
            c0 = h * MEM_HEAD_DIM
            q = q_ref[rows, c0:c0 + MEM_HEAD_DIM]
            sc = lax.dot_general(q, k_s[h], (((1,), (1,)), ((), ())), preferred_element_type=F32)
            m = jnp.max(sc, axis=1, keepdims=True)
            p = jnp.exp(sc - m)
            dsum = jnp.sum(p, axis=1, keepdims=True)
            o = jnp.dot(p.astype(BF16), v_s[h], preferred_element_type=F32)
            o_ref[rows, c0:c0 + MEM_HEAD_DIM] = (o / dsum).astype(BF16)
        return carry

    lax.fori_loop(0, seq // rb, block, 0, unroll=True)


def _memory_attention(proj, mem, mem_norm_g, w_mem_kv, mem_k_g):
    b, s, _ = proj.shape
    m, d = mem.shape[1:]
    return pl.pallas_call(
        _mem_kernel,
        grid=(b,),
        in_specs=[
            pl.BlockSpec((None, m, d), lambda i: (i, 0, 0)),
            pl.BlockSpec((1, d), lambda i: (0, 0)),
            pl.BlockSpec((d, 2 * MEM_WIDTH), lambda i: (0, 0)),
            pl.BlockSpec((1, MEM_HEAD_DIM), lambda i: (0, 0)),
            pl.BlockSpec((None, s, TN), lambda i: (i, 0, TILE_MEMQ)),
        ],
        out_specs=pl.BlockSpec((None, s, MEM_WIDTH), lambda i: (i, 0, 0)),
        out_shape=jax.ShapeDtypeStruct((b, s, MEM_WIDTH), BF16),
        scratch_shapes=[pltpu.VMEM((MEM_HEADS, m, MEM_HEAD_DIM), BF16), pltpu.VMEM((MEM_HEADS, m, MEM_HEAD_DIM), BF16)],
        compiler_params=pltpu.CompilerParams(
            dimension_semantics=("arbitrary",), vmem_limit_bytes=VMEM_LIMIT_BYTES),
        name="memory_attention",
    )(mem, mem_norm_g.reshape(1, d), w_mem_kv.astype(BF16), mem_k_g.reshape(1, -1), proj)


TM_MERGE = 1024
MC = 256


def _merge_kernel(x_ref, g1_ref, wg_ref, yd_ref, yr_ref, ym_ref, wd_ref, wr_ref, wm_ref, wo_ref, g2_ref,
                  x1_ref, h2_ref, m_s):
    d = x_ref.shape[1]
    x = x_ref[...]
    h = _rms(x, g1_ref[...]).astype(BF16)
    branches = ((yd_ref, wd_ref), (yr_ref, wr_ref), (ym_ref, wm_ref))
    for c0 in range(0, d, MC):
        merged = None
        for j, (y_ref, w_ref) in enumerate(branches):
            gate = jax.nn.sigmoid(jnp.dot(h, wg_ref[:, j * d + c0:j * d + c0 + MC], preferred_element_type=F32))
            term = gate * jnp.dot(y_ref[...], w_ref[:, c0:c0 + MC], preferred_element_type=F32)
            merged = term if merged is None else merged + term
        m_s[:, c0:c0 + MC] = merged.astype(BF16)
    x1 = x + jnp.dot(m_s[...], wo_ref[...], preferred_element_type=F32)
    x1_ref[...] = x1
    h2_ref[...] = _rms(x1, g2_ref[...]).astype(BF16)


def _merge_out(x, norm1_g, w_gate, y_dil, y_ret, y_mem, w_bd, w_br, w_bm, w_out, norm2_g):
    b, s, d = x.shape
    t = b * s
    tm = TM_MERGE
    row = lambda w: pl.BlockSpec((tm, w), lambda i: (i, 0))
    full = lambda r, c: pl.BlockSpec((r, c), lambda i: (0, 0), pipeline_mode=pl.Buffered(1))
    x1, h2 = pl.pallas_call(
        _merge_kernel,
        grid=(t // tm,),
        in_specs=[row(d), full(1, d), full(d, 3 * d), row(DIL_WIDTH), row(RET_HEADS * RET_V_DIM), row(MEM_WIDTH),
                  full(DIL_WIDTH, d), full(RET_HEADS * RET_V_DIM, d), full(MEM_WIDTH, d), full(d, d), full(1, d)],
        out_specs=[row(d), row(d)],
        out_shape=[jax.ShapeDtypeStruct((t, d), F32), jax.ShapeDtypeStruct((t, d), BF16)],
        scratch_shapes=[pltpu.VMEM((tm, d), BF16)],
        compiler_params=pltpu.CompilerParams(
            dimension_semantics=("arbitrary",), vmem_limit_bytes=VMEM_LIMIT_BYTES),
        name="merge_out_projection",
    )(x.reshape(t, d), norm1_g.reshape(1, d), w_gate.astype(BF16), y_dil.reshape(t, -1), y_ret.reshape(t, -1),
      y_mem.reshape(t, -1), w_bd.astype(BF16), w_br.astype(BF16), w_bm.astype(BF16), w_out.astype(BF16),
      norm2_g.reshape(1, d))
    return x1.reshape(b, s, d), h2.reshape(b, s, d)


TM_FFN = 1024
HALO = 16
FC = 256


def _ffn_kernel(h2_ref, hp_ref, hn_ref, x1_ref, wi_ref, cw_ref, cb_ref, wo_ref, o_ref, y_s):
    i = pl.program_id(1)
    tm = h2_ref.shape[0]
    d_ff = wo_ref.shape[0]
    h = h2_ref[...]
    prev = jnp.where(i == 0, jnp.zeros_like(hp_ref[...]), hp_ref[...])
    nxt = jnp.where(i == pl.num_programs(1) - 1, jnp.zeros_like(hn_ref[...]), hn_ref[...])
    hx = jnp.concatenate([prev, h, nxt], axis=0)
    rows = tm + 2 * HALO
    for c0 in range(0, d_ff, FC):
        u = jnp.dot(hx, wi_ref[:, c0:c0 + FC], preferred_element_type=F32)
        gt = jnp.dot(h, wi_ref[:, d_ff + c0:d_ff + c0 + FC], preferred_element_type=F32)
        cw = cw_ref[:, c0:c0 + FC]
        u_prev = pltpu.roll(u, 1, axis=0)[HALO:HALO + tm]
        u_next = pltpu.roll(u, rows - 1, axis=0)[HALO:HALO + tm]
        c = cb_ref[:, c0:c0 + FC] + u_prev * cw[0:1] + u[HALO:HALO + tm] * cw[1:2] + u_next * cw[2:3]
        y = (0.5 * c * (1.0 + lax.erf(c * (2.0 ** -0.5)))) * gt
        y_s[:, c0:c0 + FC] = y.astype(BF16)
    o_ref[...] = x1_ref[...] + jnp.dot(y_s[...], wo_ref[...], preferred_element_type=F32)


def _ffn(h2, x1, w_ffn_in, conv_w, conv_b, w_ffn_out):
    b, s, d = x1.shape
    tm = TM_FFN
    nt = s // tm
    hb = tm // HALO
    const = dict(pipeline_mode=pl.Buffered(1))
    return pl.pallas_call(
        _ffn_kernel,
        grid=(b, nt),
        in_specs=[
            pl.BlockSpec((None, tm, d), lambda j, i: (j, i, 0)),
            pl.BlockSpec((None, HALO, d), lambda j, i: (j, jnp.maximum(i * hb - 1, 0), 0)),
            pl.BlockSpec((None, HALO, d), lambda j, i: (j, jnp.minimum((i + 1) * hb, s // HALO - 1), 0)),
            pl.BlockSpec((None, tm, d), lambda j, i: (j, i, 0)),
            pl.BlockSpec((d, 2 * D_FF), lambda j, i: (0, 0), **const),
            pl.BlockSpec((3, D_FF), lambda j, i: (0, 0), **const),
            pl.BlockSpec((1, D_FF), lambda j, i: (0, 0), **const),
            pl.BlockSpec((D_FF, d), lambda j, i: (0, 0), **const),
        ],
        out_specs=pl.BlockSpec((None, tm, d), lambda j, i: (j, i, 0)),
        out_shape=jax.ShapeDtypeStruct((b, s, d), F32),
        scratch_shapes=[pltpu.VMEM((tm, D_FF), BF16)],
        compiler_params=pltpu.CompilerParams(
            dimension_semantics=("arbitrary", "arbitrary"), vmem_limit_bytes=VMEM_LIMIT_BYTES),
        name="conv_glu_ffn",
    )(h2, h2, h2, x1, w_ffn_in.astype(BF16), conv_w, conv_b.reshape(1, -1), w_ffn_out.astype(BF16))


def kernel(x, mem, norm1_g, w_in, dil_q_norm_g, dil_k_norm_g, ret_decay_logit, ret_gn_g, mem_norm_g, w_mem_kv,
           mem_q_norm_g, mem_k_norm_g, w_branch_dil, w_branch_ret, w_branch_mem, w_out, norm2_g, w_ffn_in,
           ffn_conv_w, ffn_conv_b, w_ffn_out):
    depth = w_in.shape[0]
    gate0 = DIL_COLS + RET_COLS + MEM_WIDTH
    for l in range(depth):
        w_in_l = w_in[l].astype(BF16)
        proj = _in_projection(x, norm1_g[l], w_in_l, dil_q_norm_g[l], dil_k_norm_g[l], mem_q_norm_g[l])
        y_dil = _dilated_attention(proj)
        y_ret = _retention(proj, ret_decay_logit[l], ret_gn_g[l])
        y_mem = _memory_attention(proj, mem, mem_norm_g[l], w_mem_kv[l], mem_k_norm_g[l])
        x1, h2 = _merge_out(x, norm1_g[l], w_in_l[:, gate0:], y_dil, y_ret, y_mem, w_branch_dil[l],
                            w_branch_ret[l], w_branch_mem[l], w_out[l], norm2_g[l])
        x = _ffn(h2, x1, w_ffn_in[l], ffn_conv_w[l], ffn_conv_b[l], w_ffn_out[l])
    return x
```

```python
import functools

import numpy as np
import jax
import jax.numpy as jnp
from jax import lax
from jax.experimental import pallas as pl
from jax.experimental.pallas import tpu as pltpu

F32 = jnp.float32
BF16 = jnp.bfloat16

D_MODEL = 1024
SEQ = 2048
N_MEM = 256
DIL_PATTERNS = ((128, 1), (512, 4), (2048, 16))
DIL_HEADS = 8
DIL_HEAD_DIM = 64
DIL_WIDTH = DIL_HEADS * DIL_HEAD_DIM
DIL_HALF = 64
RET_HEADS = 4
RET_QK_DIM = 64
RET_V_DIM = 128
RET_CHUNK = 128
MEM_HEADS = 4
MEM_HEAD_DIM = 128
MEM_WIDTH = MEM_HEADS * MEM_HEAD_DIM
D_FF = 2816
NORM_EPS = 1e-6
MASK_VALUE = -1e30
LOG2E = 1.4426950408889634
DIL_COLS = 3 * 3 * DIL_WIDTH
RET_COLS = 2 * RET_HEADS * RET_QK_DIM + 2 * RET_HEADS * RET_V_DIM

LANES = 128
MXU_DIM = 256
VMEM_LIMIT_BYTES = 56 * 1024 * 1024

TN = 512
SUB = 2
EPI_SCALE, EPI_NORM64, EPI_NORM128, EPI_SILU = 0, 1, 2, 3
N_TILES = 14
TILE_RET_Q, TILE_RET_K, TILE_RET_V, TILE_RET_G, TILE_MEMQ = 9, 10, 11, 12, 13
TILE_EPILOGUE = (EPI_NORM64, EPI_NORM64, EPI_SCALE) * 3 + (EPI_SCALE,) * 3 + (EPI_SILU, EPI_NORM128)
TILE_SLOT = (0, 0, 0, 1, 1, 1, 2, 2, 2) + (0,) * 5


def _rms(xf, g):
    ms = jnp.mean(xf * xf, axis=-1, keepdims=True)
    return xf * lax.rsqrt(ms + NORM_EPS) * g


def _inproj_kernel(kind_ref, slot_ref, *refs, kinds):
    n_slabs = len(refs) - 7
    xs = refs[:n_slabs]
    g1_ref, w_ref, gv_ref, bm_ref, o_ref, hs_ref, hp_ref = refs[n_slabs:]
    n = pl.program_id(1)
    seq = xs[0].shape[0]

    @pl.when(n == 0)
    def _prologue():
        g = g1_ref[...]
        rc = 256

        def normed(rows):
            return _rms(jnp.concatenate([x[rows, :] for x in xs], axis=1), g)

        def nat(i, c):
            rows = pl.ds(pl.multiple_of(i * rc, rc), rc)
            hs_ref[0, rows, :] = normed(rows).astype(BF16)
            return c

        lax.fori_loop(0, seq // rc, nat, 0, unroll=2)

        for slot in range(1, len(DIL_PATTERNS)):
            r_prev, r = DIL_PATTERNS[slot - 1][1], DIL_PATTERNS[slot][1]
            ratio = r // r_prev
            n_sub_prev, n_sub = seq // r_prev, seq // r
            rows = min(rc, n_sub)
            per = n_sub // rows
            last = slot + 1 == len(DIL_PATTERNS)

            def stage(i, c, slot=slot, ratio=ratio, n_sub_prev=n_sub_prev, n_sub=n_sub, rows=rows, per=per,
                      last=last):
                cls = i // per
                j = i % per
                src = pl.ds((cls // ratio) * n_sub_prev + cls % ratio + ratio * rows * j, rows, stride=ratio)
                dst = pl.ds(pl.multiple_of(cls * n_sub + j * rows, rows), rows)
                if slot == 1:
                    hn = normed(src)
                else:
                    hn = jnp.concatenate([hp_ref[k, src, :] for k in range(n_slabs)], axis=1)
                hs_ref[slot, dst, :] = hn.astype(BF16)
                if not last:
                    for k in range(n_slabs):
                        hp_ref[k, dst, :] = hn[:, k * LANES:(k + 1) * LANES]
                return c

            lax.fori_loop(0, r * per, stage, 0, unroll=2)

    def head_norm(acc, gv, which, head_dim):
        sq = (acc * acc).astype(BF16)
        ms = jnp.concatenate(
            [jnp.dot(sq[:, c0:c0 + MXU_DIM], bm_ref[which], preferred_element_type=F32)
             for c0 in range(0, TN, MXU_DIM)], axis=1)
        return acc * lax.rsqrt(ms * (1.0 / head_dim) + NORM_EPS) * gv

    epilogues = {
        EPI_SCALE: lambda acc, gv: acc * gv,
        EPI_NORM64: lambda acc, gv: head_norm(acc, gv, 0, DIL_HEAD_DIM),
        EPI_NORM128: lambda acc, gv: head_norm(acc, gv, 1, MEM_HEAD_DIM),
        EPI_SILU: lambda acc, gv: acc * jax.nn.sigmoid(acc),
    }
    rm = 512
    for sub in range(SUB):
        kind = kind_ref[n * SUB + sub]
        slot = slot_ref[n * SUB + sub]
        cols = slice(sub * TN, (sub + 1) * TN)
        for epi in kinds[sub]:

            @pl.when(kind == epi)
            def _(epi=epi, slot=slot, cols=cols, sub=sub):
                gv = gv_ref[sub]
                for r0 in range(0, seq, rm):
                    acc = jnp.dot(hs_ref[slot, r0:r0 + rm, :], w_ref[:, cols], preferred_element_type=F32)
                    o_ref[r0:r0 + rm, cols] = epilogues[epi](acc, gv).astype(BF16)


def _block_diag_ones(block):
    i = np.arange(MXU_DIM)
    return (i[:, None] // block == i[None, :] // block).astype(np.float32)


def _in_projection(x, norm1_g, w_in, dil_q_g, dil_k_g, mem_q_g):
    b, s, d = x.shape
    assert len(DIL_PATTERNS) == 3
    o = DIL_COLS
    wq = w_in[:, o:o + 256].reshape(d, RET_HEADS, 1, RET_QK_DIM)
    wk = w_in[:, o + 256:o + 512].reshape(d, RET_HEADS, 1, RET_QK_DIM)
    dup = lambda w: jnp.broadcast_to(w, (d, RET_HEADS, 2, RET_QK_DIM)).reshape(d, TN)
    w_all = jnp.concatenate(
        [w_in[:, :DIL_COLS], dup(wq), dup(wk), w_in[:, o + 512:o + 1536],
         w_in[:, o + RET_COLS:o + RET_COLS + MEM_WIDTH]], axis=1).astype(BF16)

    ones = jnp.ones((TN,), F32)
    gv = []
    for g in range(3):
        gv += [jnp.tile(dil_q_g[g], DIL_HEADS) * (DIL_HEAD_DIM ** -0.5 * LOG2E), jnp.tile(dil_k_g[g], DIL_HEADS),
               ones]
    gv += [ones, ones * (RET_QK_DIM ** -0.5), ones, ones, jnp.tile(mem_q_g, MEM_HEADS) * (MEM_HEAD_DIM ** -0.5)]
    gv = jnp.stack(gv).reshape(N_TILES, 1, TN).astype(F32)
    bm = jnp.asarray(np.stack([_block_diag_ones(DIL_HEAD_DIM), _block_diag_ones(MEM_HEAD_DIM)]), BF16)
    kinds = tuple(tuple(sorted(set(TILE_EPILOGUE[sub::SUB]))) for sub in range(SUB))

    grid_spec = pltpu.PrefetchScalarGridSpec(
        num_scalar_prefetch=2,
        grid=(b, N_TILES // SUB),
        in_specs=[
            pl.BlockSpec((None, s, LANES), functools.partial(lambda i, n, *_, k: (i, 0, k), k=k))
            for k in range(d // LANES)
        ] + [
            pl.BlockSpec((1, d), lambda i, n, *_: (0, 0)),
            pl.BlockSpec((d, SUB * TN), lambda i, n, *_: (0, n)),
            pl.BlockSpec((SUB, 1, TN), lambda i, n, *_: (n, 0, 0)),
            pl.BlockSpec((2, MXU_DIM, MXU_DIM), lambda i, n, *_: (0, 0, 0)),
        ],
        out_specs=pl.BlockSpec((None, s, SUB * TN), lambda i, n, *_: (i, 0, n)),
        scratch_shapes=[pltpu.VMEM((3, s, d), BF16), pltpu.VMEM((d // LANES, s, LANES), F32)],
    )
    return pl.pallas_call(
        functools.partial(_inproj_kernel, kinds=kinds),
        grid_spec=grid_spec,
        out_shape=jax.ShapeDtypeStruct((b, s, N_TILES * TN), BF16),
        compiler_params=pltpu.CompilerParams(
            dimension_semantics=("arbitrary", "arbitrary"), vmem_limit_bytes=VMEM_LIMIT_BYTES),
        name="in_projection",
    )(jnp.asarray(TILE_EPILOGUE, jnp.int32), jnp.asarray(TILE_SLOT, jnp.int32), *([x] * (d // LANES)),
      norm1_g.reshape(1, d), w_all, gv, bm)


QB = 128
KW = 256


PAIRS = 2


def _dil_kernel(slope_ref, *refs):
    for j in range(PAIRS):
        lanes = pl.ds(j * LANES, LANES)
        hp = pl.program_id(1) * PAIRS + j
        _dil_pair(slope_ref[2 * hp], slope_ref[2 * hp + 1], [r.at[:, lanes] for r in refs[:9]],
                  refs[9].at[:, lanes], refs[10 + 6 * j:16 + 6 * j], refs[10 + 6 * PAIRS + j])


def _dil_pair(s0, s1, qkv, o_ref, state_refs, bias_s):
    state = (state_refs[:3], state_refs[3:])
    seq = o_ref.shape[0]

    lane = lax.broadcasted_iota(jnp.int32, (QB, LANES), 1)
    first = lane < DIL_HEAD_DIM
    ones = jnp.ones((KW, LANES), BF16)

    for g, (_, r) in enumerate(DIL_PATTERNS):
        q_ref, k_ref, v_ref = qkv[3 * g:3 * g + 3]
        n_sub = seq // r
        per = n_sub // QB
        kw = min(KW, n_sub)
        r_prev = DIL_PATTERNS[g - 1][1] if g else 1
        ratio = r // r_prev
        src = state[(g - 1) % 2]
        dst = state[g % 2]

        row = lax.broadcasted_iota(jnp.int32, (2 * QB, kw), 0)
        col = lax.broadcasted_iota(jnp.int32, (2 * QB, kw), 1)
        base = (col - (row & (QB - 1))).astype(F32)
        slope = jnp.where(row >= QB, s1, s0) * (float(r) * LOG2E)
        offsets = (0, -DIL_HALF, -2 * DIL_HALF) if kw == KW else (0,)
        for vi, off in enumerate(offsets):
            a = jnp.abs(base + float(off))
            bias_s[vi, :, :kw] = jnp.where(a <= float(DIL_HALF), -slope * a, MASK_VALUE)

        def block(t, carry, g=g, r_prev=r_prev, ratio=ratio, n_sub=n_sub, per=per, kw=kw, src=src, dst=dst,
                  q_ref=q_ref, k_ref=k_ref, v_ref=v_ref):
            cls = t // per
            qb = t % per
            rows = pl.ds(pl.multiple_of(t * QB, QB), QB)
            if kw == KW:
                ks = jnp.clip(qb * QB - DIL_HALF, 0, n_sub - KW)
                kabs = pl.multiple_of(cls * n_sub + ks, DIL_HALF)
                vi = jnp.where(qb == 0, 0, jnp.where(qb == per - 1, 2, 1))
            else:
                kabs = pl.multiple_of(t * QB, QB)
                vi = 0
            q = q_ref[rows, :]
            kwin = k_ref[pl.ds(kabs, kw), :]
            vwin = v_ref[pl.ds(kabs, kw), :]
            zero = jnp.zeros_like(q)
            q2 = jnp.concatenate([jnp.where(first, q, zero), jnp.where(first, zero, q)], axis=0)
            sc = lax.dot_general(q2, kwin, (((1,), (1,)), ((), ())), preferred_element_type=F32)
            sc = sc + bias_s[vi, :, :kw]
            m = jnp.max(sc, axis=1, keepdims=True)
            p = jnp.exp2(sc - m).astype(BF16)
            od = jnp.dot(p, jnp.concatenate([vwin, ones[:kw]], axis=1), preferred_element_type=F32)
            o = od[:, :LANES]
            dd = od[:, LANES:]
            accb = jnp.where(first, o[:QB], o[QB:])
            mb = jnp.where(first, m[:QB], m[QB:])
            db = jnp.where(first, dd[:QB], dd[QB:])
            if g == 0:
                dst[0][rows, :] = accb
                dst[1][rows, :] = mb
                dst[2][rows, :] = db
            else:
                n_sub_prev = seq // r_prev
                start = (cls // ratio) * n_sub_prev + cls % ratio + ratio * QB * qb
                prev = pl.ds(start, QB, stride=ratio)
                m_old = src[1][prev, :]
                m_new = jnp.maximum(m_old, mb)
                a_old = jnp.exp2(m_old - m_new)
                a_blk = jnp.exp2(mb - m_new)
                dst[0][rows, :] = a_old * src[0][prev, :] + a_blk * accb
                dst[2][rows, :] = a_old * src[2][prev, :] + a_blk * db
                if g + 1 < len(DIL_PATTERNS):
                    dst[1][rows, :] = m_new
            return carry

        lax.fori_loop(0, seq // QB, block, 0, unroll=True)

    last = len(DIL_PATTERNS) - 1
    r = DIL_PATTERNS[last][1]
    per = seq // r // QB
    acc_ref, _, den_ref = state[last % 2]
    nat_ref = state[(last + 1) % 2][0]

    def unpermute(t, c):
        rows = pl.ds(pl.multiple_of(t * QB, QB), QB)
        cls = t // per
        residue = 0
        for gg in range(last, 0, -1):
            ratio = DIL_PATTERNS[gg][1] // DIL_PATTERNS[gg - 1][1]
            residue = residue + DIL_PATTERNS[gg - 1][1] * (cls % ratio)
            cls = cls // ratio
        nat_ref[pl.ds(residue + r * QB * (t % per), QB, stride=r), :] = acc_ref[rows, :] / den_ref[rows, :]
        return c

    lax.fori_loop(0, seq // QB, unpermute, 0, unroll=2)
    rc = 256

    def fin(i, c):
        rows = pl.ds(pl.multiple_of(i * rc, rc), rc)
        o_ref[rows, :] = nat_ref[rows, :].astype(BF16)
        return c

    lax.fori_loop(0, seq // rc, fin, 0, unroll=2)


def _dilated_attention(proj):
    b, s, _ = proj.shape
    assert DIL_PATTERNS[0][1] == 1 and all(w // (2 * r) == DIL_HALF for w, r in DIL_PATTERNS)
    slopes = jnp.exp2(-jnp.arange(1, DIL_HEADS + 1, dtype=F32) * (8.0 / DIL_HEADS))
    steps = DIL_HEADS // 2 // PAIRS
    width = PAIRS * LANES
    in_specs = [
        pl.BlockSpec((None, s, width), functools.partial(lambda i, hp, *_, t: (i, 0, t * steps + hp), t=t))
        for t in range(9)
    ]
    grid_spec = pltpu.PrefetchScalarGridSpec(
        num_scalar_prefetch=1,
        grid=(b, steps),
        in_specs=in_specs,
        out_specs=pl.BlockSpec((None, s, width), lambda i, hp, *_: (i, 0, hp)),
        scratch_shapes=[pltpu.VMEM((s, LANES), F32)] * (6 * PAIRS) + [pltpu.VMEM((3, 2 * QB, KW), F32)] * PAIRS,
    )
    return pl.pallas_call(
        _dil_kernel,
        grid_spec=grid_spec,
        out_shape=jax.ShapeDtypeStruct((b, s, DIL_WIDTH), BF16),
        compiler_params=pltpu.CompilerParams(
            dimension_semantics=("arbitrary", "arbitrary"), vmem_limit_bytes=VMEM_LIMIT_BYTES),
        name="dilated_attention",
    )(slopes, *([proj] * 9))


def _ret_kernel(dl_ref, qq_ref, kk_ref, v_ref, sg_ref, gn_ref, o_ref, u_s, st_s):
    for h in range(RET_HEADS):
        lanes = pl.ds(h * LANES, LANES)
        _ret_head(h, dl_ref, qq_ref.at[:, lanes], kk_ref.at[:, lanes], v_ref.at[:, lanes], sg_ref.at[:, lanes],
                  gn_ref.at[:, lanes], o_ref.at[:, lanes], u_s.at[h], st_s.at[h])


def _ret_head(h, dl_ref, qq_ref, kk_ref, v_ref, sg_ref, gn_ref, o_ref, u_s, st_s):
    seq = o_ref.shape[0]
    c = RET_CHUNK
    n_chunks = seq // c
    dk = RET_QK_DIM

    def log_sigmoid(z):
        return jnp.minimum(z, 0.0) - jnp.log1p(jnp.exp(-jnp.abs(z)))

    lgf = log_sigmoid(jnp.full((1, LANES), dl_ref[h], F32))
    lgb = log_sigmoid(jnp.full((1, LANES), dl_ref[RET_HEADS + h], F32))
    ii = lax.broadcasted_iota(jnp.int32, (c, LANES), 0).astype(F32)
    jj = lax.broadcasted_iota(jnp.int32, (c, LANES), 1).astype(F32)
    first = lax.broadcasted_iota(jnp.int32, (c, LANES), 1) < dk
    diff = ii - jj
    decay = jnp.exp(jnp.where(diff >= 0.0, lgf * diff, -lgb * diff))
    xi = jnp.exp(jnp.where(first, lgf * (ii + 1.0), lgb * (float(c) - ii)))
    zeta = jnp.exp(jnp.where(first, lgf * (float(c - 1) - ii), lgb * ii))
    chunk_f = jnp.exp(lgf * float(c))
    chunk_b = jnp.exp(lgb * float(c))

    def rows_of(n):
        return pl.ds(pl.multiple_of(n * c, c), c)

    def incr(n, carry):
        kz = (kk_ref[rows_of(n), :].astype(F32) * zeta).T.astype(BF16)
        u_s[n] = jnp.dot(kz, v_ref[rows_of(n), :], preferred_element_type=F32)
        return carry

    lax.fori_loop(0, n_chunks, incr, 0, unroll=True)

    def scan_f(n, state):
        st_s[n, :dk, :] = state.astype(BF16)
        return state * chunk_f + u_s[n, :dk, :]

    lax.fori_loop(0, n_chunks, scan_f, jnp.zeros((dk, LANES), F32))

    def scan_b(i, state):
        n = n_chunks - 1 - i
        st_s[n, dk:, :] = state.astype(BF16)
        return state * chunk_b + u_s[n, dk:, :]

    lax.fori_loop(0, n_chunks, scan_b, jnp.zeros((dk, LANES), F32))

    gn = gn_ref[...]

    def out(n, carry):
        rows = rows_of(n)
        qq = qq_ref[rows, :]
        kk = kk_ref[rows, :]
        q0 = jnp.where(first, qq, jnp.zeros_like(qq))
        a = lax.dot_general(q0, kk, (((1,), (1,)), ((), ())), preferred_element_type=F32)
        p = (a * decay).astype(BF16)
        qx = (qq.astype(F32) * xi).astype(BF16)
        y = (jnp.dot(p, v_ref[rows, :], preferred_element_type=F32)
             + jnp.dot(qx, st_s[n], preferred_element_type=F32))
        mu = jnp.mean(y, axis=-1, keepdims=True)
        yc = y - mu
        var = jnp.mean(yc * yc, axis=-1, keepdims=True)
        yn = yc * lax.rsqrt(var + NORM_EPS)
        o_ref[rows, :] = (sg_ref[rows, :].astype(F32) * (yn * gn)).astype(BF16)
        return carry

    lax.fori_loop(0, n_chunks, out, 0, unroll=True)


def _retention(proj, decay_logit, gn_g):
    b, s, _ = proj.shape
    n_chunks = s // RET_CHUNK

    def col(tile):
        return lambda i, *_: (i, 0, tile)

    width = RET_HEADS * RET_V_DIM
    assert width == TN
    grid_spec = pltpu.PrefetchScalarGridSpec(
        num_scalar_prefetch=1,
        grid=(b,),
        in_specs=[pl.BlockSpec((None, s, TN), col(t)) for t in (TILE_RET_Q, TILE_RET_K, TILE_RET_V, TILE_RET_G)]
        + [pl.BlockSpec((1, width), lambda i, *_: (0, 0))],
        out_specs=pl.BlockSpec((None, s, width), lambda i, *_: (i, 0, 0)),
        scratch_shapes=[pltpu.VMEM((RET_HEADS, n_chunks, 2 * RET_QK_DIM, RET_V_DIM), F32),
                        pltpu.VMEM((RET_HEADS, n_chunks, 2 * RET_QK_DIM, RET_V_DIM), BF16)],
    )
    return pl.pallas_call(
        _ret_kernel,
        grid_spec=grid_spec,
        out_shape=jax.ShapeDtypeStruct((b, s, width), BF16),
        compiler_params=pltpu.CompilerParams(
            dimension_semantics=("arbitrary",), vmem_limit_bytes=VMEM_LIMIT_BYTES),
        name="retention",
    )(decay_logit.astype(F32).reshape(-1), proj, proj, proj, proj, gn_g.reshape(1, -1))


def _mem_kernel(mem_ref, mg_ref, wkv_ref, kg_ref, q_ref, o_ref, k_s, v_s):
    seq = o_ref.shape[0]
    hm = _rms(mem_ref[...], mg_ref[...]).astype(BF16)
    kv = jnp.dot(hm, wkv_ref[...], preferred_element_type=F32)
    kg = kg_ref[...]
    for h in range(MEM_HEADS):
        c0 = h * MEM_HEAD_DIM
        k_s[h] = _rms(kv[:, c0:c0 + MEM_HEAD_DIM], kg).astype(BF16)
        v_s[h] = kv[:, MEM_WIDTH + c0:MEM_WIDTH + c0 + MEM_HEAD_DIM].astype(BF16)

    rb = 256

    def block(i, carry):
        rows = pl.ds(pl.multiple_of(i * rb, rb), rb)
        for h in range(MEM_HEADS):
            c0 = h * MEM_HEAD_DIM
            q = q_ref[rows, c0:c0 + MEM_HEAD_DIM]
            sc = lax.dot_general(q, k_s[h], (((1,), (1,)), ((), ())), preferred_element_type=F32)
            m = jnp.max(sc, axis=1, keepdims=True)
            p = jnp.exp(sc - m)
            dsum = jnp.sum(p, axis=1, keepdims=True)
            o = jnp.dot(p.astype(BF16), v_s[h], preferred_element_type=F32)
            o_ref[rows, c0:c0 + MEM_HEAD_DIM] = (o / dsum).astype(BF16)
        return carry

    lax.fori_loop(0, seq // rb, block, 0, unroll=True)


def _ret_mem_kernel(dl_ref, qq_ref, kk_ref, v_ref, sg_ref, gn_ref, mem_ref, mg_ref, wkv_ref, kg_ref, q_ref,
                    o_ret_ref, o_mem_ref, u_s, st_s, k_s, v_s):
    _ret_kernel(dl_ref, qq_ref, kk_ref, v_ref, sg_ref, gn_ref, o_ret_ref, u_s, st_s)
    _mem_kernel(mem_ref, mg_ref, wkv_ref, kg_ref, q_ref, o_mem_ref, k_s, v_s)


def _retention_and_memory(proj, decay_logit, gn_g, mem, mem_norm_g, w_mem_kv, mem_k_g):
    b, s, _ = proj.shape
    m, d = mem.shape[1:]
    n_chunks = s // RET_CHUNK
    width = RET_HEADS * RET_V_DIM
    assert width == TN and MEM_WIDTH == TN
    tile = lambda t: pl.BlockSpec((None, s, TN), functools.partial(lambda i, *_, t: (i, 0, t), t=t))
    const = lambda r, c: pl.BlockSpec((r, c), lambda i, *_: (0, 0))
    grid_spec = pltpu.PrefetchScalarGridSpec(
        num_scalar_prefetch=1,
        grid=(b,),
        in_specs=[tile(TILE_RET_Q), tile(TILE_RET_K), tile(TILE_RET_V), tile(TILE_RET_G), const(1, width),
                  pl.BlockSpec((None, m, d), lambda i, *_: (i, 0, 0)), const(1, d), const(d, 2 * MEM_WIDTH),
                  const(1, MEM_HEAD_DIM), tile(TILE_MEMQ)],
        out_specs=[pl.BlockSpec((None, s, TN), lambda i, *_: (i, 0, 0))] * 2,
        scratch_shapes=[pltpu.VMEM((RET_HEADS, n_chunks, 2 * RET_QK_DIM, RET_V_DIM), F32),
                        pltpu.VMEM((RET_HEADS, n_chunks, 2 * RET_QK_DIM, RET_V_DIM), BF16),
                        pltpu.VMEM((MEM_HEADS, m, MEM_HEAD_DIM), BF16), pltpu.VMEM((MEM_HEADS, m, MEM_HEAD_DIM), BF16)],
    )
    return pl.pallas_call(
        _ret_mem_kernel,
        grid_spec=grid_spec,
        out_shape=[jax.ShapeDtypeStruct((b, s, TN), BF16)] * 2,
        compiler_params=pltpu.CompilerParams(
            dimension_semantics=("arbitrary",), vmem_limit_bytes=VMEM_LIMIT_BYTES),
        name="retention_and_memory",
    )(decay_logit.astype(F32).reshape(-1), proj, proj, proj, proj, gn_g.reshape(1, -1), mem,
      mem_norm_g.reshape(1, d), w_mem_kv.astype(BF16), mem_k_g.reshape(1, -1), proj)


def _memory_attention(proj, mem, mem_norm_g, w_mem_kv, mem_k_g):
    b, s, _ = proj.shape
    m, d = mem.shape[1:]
    return pl.pallas_call(
        _mem_kernel,
        grid=(b,),
        in_specs=[
            pl.BlockSpec((None, m, d), lambda i: (i, 0, 0)),
            pl.BlockSpec((1, d), lambda i: (0, 0)),
            pl.BlockSpec((d, 2 * MEM_WIDTH), lambda i: (0, 0)),
            pl.BlockSpec((1, MEM_HEAD_DIM), lambda i: (0, 0)),
            pl.BlockSpec((None, s, TN), lambda i: (i, 0, TILE_MEMQ)),
        ],
        out_specs=pl.BlockSpec((None, s, MEM_WIDTH), lambda i: (i, 0, 0)),
        out_shape=jax.ShapeDtypeStruct((b, s, MEM_WIDTH), BF16),
        scratch_shapes=[pltpu.VMEM((MEM_HEADS, m, MEM_HEAD_DIM), BF16), pltpu.VMEM((MEM_HEADS, m, MEM_HEAD_DIM), BF16)],
        compiler_params=pltpu.CompilerParams(
            dimension_semantics=("arbitrary",), vmem_limit_bytes=VMEM_LIMIT_BYTES),
        name="memory_attention",
    )(mem, mem_norm_g.reshape(1, d), w_mem_kv.astype(BF16), mem_k_g.reshape(1, -1), proj)


TM_MERGE = 1024
MC = 256


def _merge_kernel(x_ref, g1_ref, wg_ref, yd_ref, yr_ref, ym_ref, wd_ref, wr_ref, wm_ref, wo_ref, g2_ref,
                  x1_ref, h2_ref, m_s):
    d = x_ref.shape[1]
    x = x_ref[...]
    h = _rms(x, g1_ref[...]).astype(BF16)
    branches = ((yd_ref, wd_ref), (yr_ref, wr_ref), (ym_ref, wm_ref))
    for c0 in range(0, d, MC):
        merged = None
        for j, (y_ref, w_ref) in enumerate(branches):
            gate = jax.nn.sigmoid(jnp.dot(h, wg_ref[:, j * d + c0:j * d + c0 + MC], preferred_element_type=F32))
            term = gate * jnp.dot(y_ref[...], w_ref[:, c0:c0 + MC], preferred_element_type=F32)
            merged = term if merged is None else merged + term
        m_s[:, c0:c0 + MC] = merged.astype(BF16)
    x1 = x + jnp.dot(m_s[...], wo_ref[...], preferred_element_type=F32)
    x1_ref[...] = x1
    h2_ref[...] = _rms(x1, g2_ref[...]).astype(BF16)


def _merge_out(x, norm1_g, w_gate, y_dil, y_ret, y_mem, w_bd, w_br, w_bm, w_out, norm2_g):
    b, s, d = x.shape
    t = b * s
    tm = TM_MERGE
    row = lambda w: pl.BlockSpec((tm, w), lambda i: (i, 0))
    full = lambda r, c: pl.BlockSpec((r, c), lambda i: (0, 0), pipeline_mode=pl.Buffered(1))
    x1, h2 = pl.pallas_call(
        _merge_kernel,
        grid=(t // tm,),
        in_specs=[row(d), full(1, d), full(d, 3 * d), row(DIL_WIDTH), row(RET_HEADS * RET_V_DIM), row(MEM_WIDTH),
                  full(DIL_WIDTH, d), full(RET_HEADS * RET_V_DIM, d), full(MEM_WIDTH, d), full(d, d), full(1, d)],
        out_specs=[row(d), row(d)],
        out_shape=[jax.ShapeDtypeStruct((t, d), F32), jax.ShapeDtypeStruct((t, d), BF16)],
        scratch_shapes=[pltpu.VMEM((tm, d), BF16)],
        compiler_params=pltpu.CompilerParams(
            dimension_semantics=("arbitrary",), vmem_limit_bytes=VMEM_LIMIT_BYTES),
        name="merge_out_projection",
    )(x.reshape(t, d), norm1_g.reshape(1, d), w_gate.astype(BF16), y_dil.reshape(t, -1), y_ret.reshape(t, -1),
      y_mem.reshape(t, -1), w_bd.astype(BF16), w_br.astype(BF16), w_bm.astype(BF16), w_out.astype(BF16),
      norm2_g.reshape(1, d))
    return x1.reshape(b, s, d), h2.reshape(b, s, d)


TM_FFN = 1024
HALO = 16
FC = 256


def _ffn_kernel(h2_ref, hp_ref, hn_ref, x1_ref, wi_ref, cw_ref, cb_ref, wo_ref, o_ref, y_s):
    i = pl.program_id(1)
    tm = h2_ref.shape[0]
    d_ff = wo_ref.shape[0]
    h = h2_ref[...]
    prev = jnp.where(i == 0, jnp.zeros_like(hp_ref[...]), hp_ref[...])
    nxt = jnp.where(i == pl.num_programs(1) - 1, jnp.zeros_like(hn_ref[...]), hn_ref[...])
    hx = jnp.concatenate([prev, h, nxt], axis=0)
    rows = tm + 2 * HALO
    for c0 in range(0, d_ff, FC):
        u = jnp.dot(hx, wi_ref[:, c0:c0 + FC], preferred_element_type=F32)
        gt = jnp.dot(h, wi_ref[:, d_ff + c0:d_ff + c0 + FC], preferred_element_type=F32)
        cw = cw_ref[:, c0:c0 + FC]
        u_prev = pltpu.roll(u, 1, axis=0)[HALO:HALO + tm]
        u_next = pltpu.roll(u, rows - 1, axis=0)[HALO:HALO + tm]
        c = cb_ref[:, c0:c0 + FC] + u_prev * cw[0:1] + u[HALO:HALO + tm] * cw[1:2] + u_next * cw[2:3]
        y = (0.5 * c * (1.0 + lax.erf(c * (2.0 ** -0.5)))) * gt
        y_s[:, c0:c0 + FC] = y.astype(BF16)
    o_ref[...] = x1_ref[...] + jnp.dot(y_s[...], wo_ref[...], preferred_element_type=F32)


def _ffn(h2, x1, w_ffn_in, conv_w, conv_b, w_ffn_out):
    b, s, d = x1.shape
    tm = TM_FFN
    nt = s // tm
    hb = tm // HALO
    const = dict(pipeline_mode=pl.Buffered(1))
    return pl.pallas_call(
        _ffn_kernel,
        grid=(b, nt),
        in_specs=[
            pl.BlockSpec((None, tm, d), lambda j, i: (j, i, 0)),
            pl.BlockSpec((None, HALO, d), lambda j, i: (j, jnp.maximum(i * hb - 1, 0), 0)),
            pl.BlockSpec((None, HALO, d), lambda j, i: (j, jnp.minimum((i + 1) * hb, s // HALO - 1), 0)),
            pl.BlockSpec((None, tm, d), lambda j, i: (j, i, 0)),
            pl.BlockSpec((d, 2 * D_FF), lambda j, i: (0, 0), **const),
            pl.BlockSpec((3, D_FF), lambda j, i: (0, 0), **const),
            pl.BlockSpec((1, D_FF), lambda j, i: (0, 0), **const),
            pl.BlockSpec((D_FF, d), lambda j, i: (0, 0), **const),
        ],
        out_specs=pl.BlockSpec((None, tm, d), lambda j, i: (j, i, 0)),
        out_shape=jax.ShapeDtypeStruct((b, s, d), F32),
        scratch_shapes=[pltpu.VMEM((tm, D_FF), BF16)],
        compiler_params=pltpu.CompilerParams(
            dimension_semantics=("arbitrary", "arbitrary"), vmem_limit_bytes=VMEM_LIMIT_BYTES),
        name="conv_glu_ffn",
    )(h2, h2, h2, x1, w_ffn_in.astype(BF16), conv_w, conv_b.reshape(1, -1), w_ffn_out.astype(BF16))


def kernel(x, mem, norm1_g, w_in, dil_q_norm_g, dil_k_norm_g, ret_decay_logit, ret_gn_g, mem_norm_g, w_mem_kv,
           mem_q_norm_g, mem_k_norm_g, w_branch_dil, w_branch_ret, w_branch_mem, w_out, norm2_g, w_ffn_in,
           ffn_conv_w, ffn_conv_b, w_ffn_out):
    depth = w_in.shape[0]
    gate0 = DIL_COLS + RET_COLS + MEM_WIDTH
    for l in range(depth):
        w_in_l = w_in[l].astype(BF16)
        proj = _in_projection(x, norm1_g[l], w_in_l, dil_q_norm_g[l], dil_k_norm_g[l], mem_q_norm_g[l])
        y_dil = _dilated_attention(proj)
        y_ret, y_mem = _retention_and_memory(proj, ret_decay_logit[l], ret_gn_g[l], mem, mem_norm_g[l],
                                             w_mem_kv[l], mem_k_norm_g[l])
        x1, h2 = _merge_out(x, norm1_g[l], w_in_l[:, gate0:], y_dil, y_ret, y_mem, w_branch_dil[l],
                            w_branch_ret[l], w_branch_mem[l], w_out[l], norm2_g[l])
        x = _ffn(h2, x1, w_ffn_in[l], ffn_conv_w[l], ffn_conv_b[l], w_ffn_out[l])
    return x
```
